```python
import jax, jax.numpy as jnp
from jax import lax
import numpy as np

D_MODEL = 1024
BATCH = 8
SEQ = 8192
DEPTH = 1
DEC_BATCH = 8
DEC_SEQ = 16
PAST_LEN = 2048

CHUNK = 64
D_CONV = 1024
CONV_W = 3
N_HEADS = 8
N_KV_HEADS = 2
HEAD_DIM = 128
Q_PER_KV = N_HEADS // N_KV_HEADS
N_IDX_HEADS = 16
IDX_DIM = 64
TOPK_MAX = 256
PEER_HEADS = 8
PEER_QDIM = 256
PEER_HALF = PEER_QDIM // 2
N_KEYS = 128
N_EXPERTS = N_KEYS * N_KEYS
PEER_TOPK = 16
PEER_BLOCK = 128
LN_EPS = 1e-5
RMS_EPS = 1e-6
DEEPNORM_ALPHA = (2 * DEPTH) ** 0.25
DEEPNORM_BETA = (8 * DEPTH) ** -0.25
MIX_WIDTHS = (D_CONV, D_CONV, D_CONV,
              N_HEADS * HEAD_DIM, N_KV_HEADS * HEAD_DIM, N_KV_HEADS * HEAD_DIM,
              N_IDX_HEADS * IDX_DIM, IDX_DIM, N_IDX_HEADS,
              D_MODEL, D_MODEL)
MIX_IN_WIDTH = sum(MIX_WIDTHS)
V_COL_START = 3 * D_CONV + (N_HEADS + N_KV_HEADS) * HEAD_DIM
V_COL_WIDTH = N_KV_HEADS * HEAD_DIM

kernel_name = "chunk_causal_conv_dsa_peer_encoder_step"


def _layer_norm(x, g, b):
    xf = x.astype(jnp.float32)
    mu = jnp.mean(xf, axis=-1, keepdims=True)
    var = jnp.mean(jnp.square(xf - mu), axis=-1, keepdims=True)
    return ((xf - mu) * lax.rsqrt(var + LN_EPS) * g + b).astype(x.dtype)


def _split_cols(z):
    out, start = [], 0
    for w in MIX_WIDTHS:
        out.append(z[..., start:start + w])
        start += w
    return out


def _short_conv(u, hist, conv_w):
    t = u.shape[1]
    up = jnp.concatenate([hist, u], axis=1)
    y = conv_w[0] * up[:, 0:t]
    for j in range(1, CONV_W):
        y = y + conv_w[j] * up[:, j:j + t]
    return y, up[:, t:]


def _sparse_attend(q, qi, wi, k, v, ki, limit, k_sel):
    s_len = k.shape[1]
    pos = jnp.arange(s_len)
    rel = jax.nn.relu(jnp.einsum('bqhd,bsd->bqhs', qi, ki))
    score = jnp.einsum('bqhs,bqh->bqs', rel, wi).astype(jnp.float32)
    score = jnp.where(pos[None, None, :] < limit, score, -jnp.inf)
    _, sel = lax.top_k(score, k_sel)
    valid = sel < limit
    kg = jax.vmap(lambda kk, ii: kk[ii])(k, sel)
    vg = jax.vmap(lambda vv, ii: vv[ii])(v, sel)
    b, nq = q.shape[0], q.shape[1]
    qg = q.reshape(b, nq, N_KV_HEADS, Q_PER_KV, HEAD_DIM)
    logits = jnp.einsum('bqgrd,bqkgd->bqgrk', qg, kg).astype(jnp.float32) * (HEAD_DIM ** -0.5)
    logits = jnp.where(valid[:, :, None, None, :], logits, -jnp.inf)
    p = jax.nn.softmax(logits, axis=-1).astype(v.dtype)
    o = jnp.einsum('bqgrk,bqkgd->bqgrd', p, vg)
    return o.reshape(b, nq, N_HEADS, HEAD_DIM)


def _prompt_attention(q, k, v, qi, ki, wi):
    b, t = q.shape[0], q.shape[1]
    nb = t // CHUNK
    k_sel = min(TOPK_MAX, t // 4)

    def to_blocks(a):
        return jnp.moveaxis(a.reshape((b, nb, CHUNK) + a.shape[2:]), 1, 0)

    def one_block(args):
        n, qb, qib, wib = args
        return _sparse_attend(qb, qib, wib, k, v, ki, (n + 1) * CHUNK, k_sel)

    o = lax.map(one_block, (jnp.arange(nb), to_blocks(q), to_blocks(qi), to_blocks(wi)))
    return jnp.moveaxis(o, 0, 1).reshape(b, t, N_HEADS, HEAD_DIM)


def _peer_block(h, w_peer_q, peer_q_gain, sub_keys_1, sub_keys_2, expert_u, expert_v):
    n = h.shape[0]
    q = (h @ w_peer_q).reshape(n, PEER_HEADS, PEER_QDIM)
    qf = q.astype(jnp.float32)
    q = (qf * lax.rsqrt(jnp.mean(jnp.square(qf), axis=-1, keepdims=True) + RMS_EPS) * peer_q_gain).astype(h.dtype)
    s1 = jnp.einsum('nhd,hkd->nhk', q[..., :PEER_HALF], sub_keys_1)
    s2 = jnp.einsum('nhd,hkd->nhk', q[..., PEER_HALF:], sub_keys_2)
    v1, i1 = lax.top_k(s1, PEER_TOPK)
    v2, i2 = lax.top_k(s2, PEER_TOPK)
    cand = (v1[..., :, None] + v2[..., None, :]).reshape(n, PEER_HEADS, PEER_TOPK * PEER_TOPK)
    cidx = (i1[..., :, None] * N_KEYS + i2[..., None, :]).reshape(n, PEER_HEADS, PEER_TOPK * PEER_TOPK)
    cs, ci = lax.top_k(cand, PEER_TOPK)
    eidx = jnp.take_along_axis(cidx, ci, axis=-1)
    g = jax.nn.softmax(cs.astype(jnp.float32), axis=-1).astype(h.dtype)
    u = expert_u[eidx]
    a = jax.nn.gelu(jnp.einsum('nd,nhkd->nhk', h, u))
    vv = expert_v[eidx]
    return jnp.einsum('nhk,nhkd->nd', g * a, vv)


def _peer(h, w_peer_q, peer_q_gain, sub_keys_1, sub_keys_2, expert_u, expert_v):
    b, t, d = h.shape
    n = b * t
    nblk = -(-n // PEER_BLOCK)
    hf = jnp.pad(h.reshape(n, d), ((0, nblk * PEER_BLOCK - n), (0, 0)))
    out = lax.map(lambda hb: _peer_block(hb, w_peer_q, peer_q_gain, sub_keys_1, sub_keys_2, expert_u, expert_v),
                  hf.reshape(nblk, PEER_BLOCK, d))
    return out.reshape(nblk * PEER_BLOCK, d)[:n].reshape(b, t, d)


def _layer(x, c, conv_hist, past, w_ada, b_ada, w_mix_in, conv_w, w_conv_out, w_attn_out, w_o,
           ln1_g, ln1_b, w_peer_q, peer_q_gain, sub_keys_1, sub_keys_2, expert_u, expert_v, ln2_g, ln2_b):
    b, t, _ = x.shape
    mod = jax.nn.silu(c) @ w_ada + b_ada
    sh1, sc1, g1, sh2, sc2, g2 = [m[:, None, :] for m in jnp.split(mod, 6, axis=-1)]

    h = x * (1 + sc1) + sh1
    xin, gb, gc, q, k, v, qi, ki, wi, gate_a, gate_b = _split_cols(h @ w_mix_in)
    conv_y, conv_state = _short_conv(gc * xin, conv_hist, conv_w)
    y_a = (gb * conv_y) @ w_conv_out
    q = q.reshape(b, t, N_HEADS, HEAD_DIM)
    k = k.reshape(b, t, N_KV_HEADS, HEAD_DIM)
    v = v.reshape(b, t, N_KV_HEADS, HEAD_DIM)
    qi = qi.reshape(b, t, N_IDX_HEADS, IDX_DIM)
    if past is None:
        o = _prompt_attention(q, k, v, qi, ki, wi)
    else:
        past_k, past_v, past_ki = past
        k_all = jnp.concatenate([past_k, k], axis=1)
        v_all = jnp.concatenate([past_v, v], axis=1)
        ki_all = jnp.concatenate([past_ki, ki], axis=1)
        l_keys = k_all.shape[1]
        o = _sparse_attend(q, qi, wi, k_all, v_all, ki_all, l_keys, min(TOPK_MAX, l_keys // 4))
    y_b = o.reshape(b, t, N_HEADS * HEAD_DIM) @ w_attn_out
    merged = jax.nn.sigmoid(gate_a) * y_a + jax.nn.sigmoid(gate_b) * y_b
    x = _layer_norm(DEEPNORM_ALPHA * x + g1 * (merged @ w_o), ln1_g, ln1_b)

    h2 = x * (1 + sc2) + sh2
    ff = _peer(h2, w_peer_q, peer_q_gain, sub_keys_1, sub_keys_2, expert_u, expert_v)
    x = _layer_norm(DEEPNORM_ALPHA * x + g2 * ff, ln2_g, ln2_b)
    return x, k, v, ki, conv_state


def setup_inputs(seed: int = 0) -> dict:
    key = jax.random.key(seed)
    ks = jax.random.split(key, 32)
    f32 = jnp.float32

    def nrm(k, shape, s=1.0):
        return jax.random.normal(k, shape, f32) * s

    col_scale = jnp.ones((MIX_IN_WIDTH,), f32).at[V_COL_START:V_COL_START + V_COL_WIDTH].set(DEEPNORM_BETA)
    return {
        "x_prompt": nrm(ks[0], (BATCH, SEQ, D_MODEL)),
        "x_sample": nrm(ks[1], (DEC_BATCH, DEC_SEQ, D_MODEL)),
        "c_prompt": nrm(ks[2], (BATCH, D_MODEL)),
        "c_sample": nrm(ks[3], (DEC_BATCH, D_MODEL)),
        "cache_k": nrm(ks[4], (DEPTH, DEC_BATCH, PAST_LEN, N_KV_HEADS, HEAD_DIM)),
        "cache_v": nrm(ks[5], (DEPTH, DEC_BATCH, PAST_LEN, N_KV_HEADS, HEAD_DIM), DEEPNORM_BETA),
        "cache_idx_k": nrm(ks[6], (DEPTH, DEC_BATCH, PAST_LEN, IDX_DIM)),
        "state_conv": nrm(ks[7], (DEPTH, DEC_BATCH, CONV_W - 1, D_CONV), 0.5),
        "w_ada": nrm(ks[8], (DEPTH, D_MODEL, 6 * D_MODEL), 0.5 * D_MODEL ** -0.5),
        "b_ada": nrm(ks[9], (DEPTH, 6 * D_MODEL), 0.01),
        "w_mix_in": nrm(ks[10], (DEPTH, D_MODEL, MIX_IN_WIDTH), D_MODEL ** -0.5) * col_scale,
        "conv_w": nrm(ks[11], (DEPTH, CONV_W, D_CONV), CONV_W ** -0.5),
        "w_conv_out": nrm(ks[12], (DEPTH, D_CONV, D_MODEL), DEEPNORM_BETA * D_CONV ** -0.5),
        "w_attn_out": nrm(ks[13], (DEPTH, N_HEADS * HEAD_DIM, D_MODEL), DEEPNORM_BETA * (N_HEADS * HEAD_DIM) ** -0.5),
        "w_o": nrm(ks[14], (DEPTH, D_MODEL, D_MODEL), DEEPNORM_BETA * D_MODEL ** -0.5),
        "ln1_g": 1.0 + nrm(ks[15], (DEPTH, D_MODEL), 0.02),
        "ln1_b": nrm(ks[16], (DEPTH, D_MODEL), 0.02),
        "w_peer_q": nrm(ks[17], (DEPTH, D_MODEL, PEER_HEADS * PEER_QDIM), D_MODEL ** -0.5),
        "peer_q_gain": 1.0 + nrm(ks[18], (DEPTH, PEER_HEADS, PEER_QDIM), 0.02),
        "sub_keys_1": nrm(ks[19], (DEPTH, PEER_HEADS, N_KEYS, PEER_HALF), PEER_HALF ** -0.5),
        "sub_keys_2": nrm(ks[20], (DEPTH, PEER_HEADS, N_KEYS, PEER_HALF), PEER_HALF ** -0.5),
        "expert_u": nrm(ks[21], (DEPTH, N_EXPERTS, D_MODEL), D_MODEL ** -0.5),
        "expert_v": nrm(ks[22], (DEPTH, N_EXPERTS, D_MODEL), DEEPNORM_BETA),
        "ln2_g": 1.0 + nrm(ks[23], (DEPTH, D_MODEL), 0.02),
        "ln2_b": nrm(ks[24], (DEPTH, D_MODEL), 0.02),
    }


def reference(x_prompt, x_sample, c_prompt, c_sample, cache_k, cache_v, cache_idx_k, state_conv,
              w_ada, b_ada, w_mix_in, conv_w, w_conv_out, w_attn_out, w_o, ln1_g, ln1_b,
              w_peer_q, peer_q_gain, sub_keys_1, sub_keys_2, expert_u, expert_v, ln2_g, ln2_b):
    y_p, y_s = x_prompt, x_sample
    kp_l, vp_l, kip_l, cp_l = [], [], [], []
    ks_l, vs_l, kis_l, cs_l = [], [], [], []
    for l in range(DEPTH):
        params = (w_ada[l], b_ada[l], w_mix_in[l], conv_w[l], w_conv_out[l], w_attn_out[l], w_o[l],
                  ln1_g[l], ln1_b[l], w_peer_q[l], peer_q_gain[l], sub_keys_1[l], sub_keys_2[l],
                  expert_u[l], expert_v[l], ln2_g[l], ln2_b[l])
        zero_hist = jnp.zeros((y_p.shape[0], CONV_W - 1, D_CONV), y_p.dtype)
        y_p, kp, vp, kip, cp = _layer(y_p, c_prompt, zero_hist, None, *params)
        y_s, kss, vss, kiss, css = _layer(y_s, c_sample, state_conv[l],
                                          (cache_k[l], cache_v[l], cache_idx_k[l]), *params)
        kp_l.append(kp); vp_l.append(vp); kip_l.append(kip); cp_l.append(cp)
        ks_l.append(kss); vs_l.append(vss); kis_l.append(kiss); cs_l.append(css)
    return (y_p, y_s,
            jnp.stack(kp_l), jnp.stack(vp_l), jnp.stack(kip_l), jnp.stack(cp_l),
            jnp.stack(ks_l), jnp.stack(vs_l), jnp.stack(kis_l), jnp.stack(cs_l))
```

```python
import functools

import jax
import jax.numpy as jnp
from jax import lax
from jax.experimental import pallas as pl
from jax.experimental.pallas import tpu as pltpu

BF = jnp.bfloat16
F32 = jnp.float32
I32 = jnp.int32

D = 1024
DC = 1024
CHUNK = 64
NH = 8
NKV = 2
QPK = NH // NKV
HD = 128
NIH = 16
IDIM = 64
TOPK_MAX = 256
PH = 8
PQ = 256
PHALF = PQ // 2
NKEYS = 128
NEXP = NKEYS * NKEYS
PTOP = 16
LN_EPS = 1e-5
RMS_EPS = 1e-6
ALPHA = 2.0 ** 0.25

LANES = 128
SUBLANES = 8
VMEM_LIMIT = 56 * 1024 * 1024
INT_MIN = -2 ** 31
NEG_BIAS = -3e30
M_INIT = -1e30

_NT = (((1,), (1,)), ((), ()))


def _params(sem):
    return pltpu.CompilerParams(dimension_semantics=sem, vmem_limit_bytes=VMEM_LIMIT)


def _layer_norm(y, g, b):
    mu = jnp.mean(y, axis=-1, keepdims=True)
    yc = y - mu
    var = jnp.mean(yc * yc, axis=-1, keepdims=True)
    return yc * lax.rsqrt(var + LN_EPS) * g + b


def _mod_kernel(c_ref, w_ref, b_ref, o_ref):
    c = c_ref[...]
    s = c * jax.nn.sigmoid(c)
    o_ref[...] = jnp.dot(s.astype(BF), w_ref[...], preferred_element_type=F32) + b_ref[...]


def _modulation(c, w_ada_bf, b_ada):
    nb = c.shape[0]
    n_out = w_ada_bf.shape[1]
    tn = D
    return pl.pallas_call(
        _mod_kernel,
        grid=(n_out // tn,),
        in_specs=[pl.BlockSpec((nb, D), lambda j: (0, 0)),
                  pl.BlockSpec((D, tn), lambda j: (0, j)),
                  pl.BlockSpec((1, tn), lambda j: (0, j))],
        out_specs=pl.BlockSpec((nb, tn), lambda j: (0, j)),
        out_shape=jax.ShapeDtypeStruct((nb, n_out), F32),
        compiler_params=_params(("arbitrary",)),
        name="modulation",
    )(c, w_ada_bf, b_ada.reshape(1, n_out))


def _mod_spec(k, grid_rank):
    if grid_rank == 2:
        return pl.BlockSpec((None, None, 1, D), lambda b, t: (b, k, 0, 0))
    raise ValueError(grid_rank)


def _conv_kernel(x_ref, sc_ref, sh_ref, hist_ref, w3_ref, cw_ref, wco_ref, ya_ref, cs_ref, ubuf, *, tm):
    t = pl.program_id(1)
    h = (x_ref[...] * (1.0 + sc_ref[...]) + sh_ref[...]).astype(BF)
    z = jnp.dot(h, w3_ref[...], preferred_element_type=F32)
    xin = z[:, :DC]
    gb = z[:, DC:2 * DC]
    gc = z[:, 2 * DC:]

    @pl.when(t == 0)
    def _():
        ubuf[0:SUBLANES, :] = jnp.zeros((SUBLANES, DC), F32)
        ubuf[SUBLANES - 2:SUBLANES, :] = hist_ref[...]

    ubuf[SUBLANES:SUBLANES + tm, :] = gc * xin
    cw = cw_ref[...]
    y = (cw[0:1] * ubuf[SUBLANES - 2:SUBLANES - 2 + tm, :]
         + cw[1:2] * ubuf[SUBLANES - 1:SUBLANES - 1 + tm, :]
         + cw[2:3] * ubuf[SUBLANES:SUBLANES + tm, :])
    ya_ref[...] = jnp.dot((gb * y).astype(BF), wco_ref[...], preferred_element_type=F32)
    tail = ubuf[tm:tm + SUBLANES, :]
    ubuf[0:SUBLANES, :] = tail
    cs_ref[...] = tail[SUBLANES - 2:SUBLANES]


def _conv_mixer(x, mod4, hist, w3_bf, conv_w, wco_bf, *, tm):
    b, t, _ = x.shape
    return pl.pallas_call(
        functools.partial(_conv_kernel, tm=tm),
        grid=(b, t // tm),
        in_specs=[pl.BlockSpec((None, tm, D), lambda i, j: (i, j, 0)),
                  _mod_spec(1, 2), _mod_spec(0, 2),
                  pl.BlockSpec((None, 2, DC), lambda i, j: (i, 0, 0)),
                  pl.BlockSpec((D, 3 * DC), lambda i, j: (0, 0)),
                  pl.BlockSpec((3, DC), lambda i, j: (0, 0)),
                  pl.BlockSpec((DC, D), lambda i, j: (0, 0))],
        out_specs=[pl.BlockSpec((None, tm, D), lambda i, j: (i, j, 0)),
                   pl.BlockSpec((None, 2, DC), lambda i, j: (i, 0, 0))],
        out_shape=[jax.ShapeDtypeStruct((b, t, D), F32),
                   jax.ShapeDtypeStruct((b, 2, DC), F32)],
        scratch_shapes=[pltpu.VMEM((tm + SUBLANES, DC), F32)],
        compiler_params=_params(("arbitrary", "arbitrary")),
        name="conv_mixer",
    )(x, mod4, mod4, hist, w3_bf, conv_w, wco_bf)


_QW = NH * HD
_KW = NKV * HD
_PROJ_COLS = (0, _QW, _QW + _KW, _QW + 2 * _KW, 2 * _QW + 2 * _KW,
              2 * _QW + 2 * _KW + LANES, 2 * _QW + 2 * _KW + 2 * LANES)


def _proj_kernel(x_ref, sc_ref, sh_ref, w_ref, q_ref, k_ref, v_ref, kb_ref, vb_ref, qi_ref, ki_ref, kib_ref, wi_ref):
    h = (x_ref[...] * (1.0 + sc_ref[...]) + sh_ref[...]).astype(BF)
    z = jnp.dot(h, w_ref[...], preferred_element_type=F32)
    c = _PROJ_COLS
    q_ref[...] = z[:, c[0]:c[1]].astype(BF)
    k = z[:, c[1]:c[2]]
    v = z[:, c[2]:c[3]]
    k_ref[...] = k
    v_ref[...] = v
    kb_ref[...] = k.astype(BF)
    vb_ref[...] = v.astype(BF)
    qi_ref[...] = z[:, c[3]:c[4]].astype(BF)
    ki = z[:, c[4]:c[4] + IDIM]
    ki_ref[...] = ki
    kib_ref[...] = ki.astype(BF)
    wi_ref[...] = z[:, c[5]:c[5] + NIH]


def _projections(x, mod4, wp_bf, *, tm):
    b, t, _ = x.shape
    widths = (_QW, _KW, _KW, _KW, _KW, NIH * IDIM, IDIM, IDIM, NIH)
    dtypes = (BF, F32, F32, BF, BF, BF, F32, BF, F32)
    return pl.pallas_call(
        _proj_kernel,
        grid=(b, t // tm),
        in_specs=[pl.BlockSpec((None, tm, D), lambda i, j: (i, j, 0)),
                  _mod_spec(1, 2), _mod_spec(0, 2),
                  pl.BlockSpec(wp_bf.shape, lambda i, j: (0, 0))],
        out_specs=[pl.BlockSpec((None, tm, w), lambda i, j: (i, j, 0)) for w in widths],
        out_shape=[jax.ShapeDtypeStruct((b, t, w), dt) for w, dt in zip(widths, dtypes)],
        compiler_params=_params(("arbitrary", "arbitrary")),
        name="projections",
    )(x, mod4, mod4, wp_bf)


def _attn_kernel(lim_ref, qi_ref, wi_ref, q_ref, kit_ref, k_ref, v_ref, o_ref,
                 key_scr, wb_scr, m_scr, l_scr, acc_scr, *, tq, tk, k_sel):
    n = pl.program_id(1)
    limit = lim_ref[n]
    nt = lax.div(limit + (tk - 1), tk)
    scale = HD ** -0.5

    wi = wi_ref[...]
    for h in range(NIH):
        wb_scr[h] = jnp.broadcast_to(wi[:, h:h + 1], (tq, tk))
    qi = qi_ref[...]

    def score_tile(t, carry):
        x = jnp.dot(qi, kit_ref[t], preferred_element_type=F32)
        s = wb_scr[0] * jnp.maximum(x[0:tq], 0.0)
        for h in range(1, NIH):
            s = s + wb_scr[h] * jnp.maximum(x[h * tq:(h + 1) * tq], 0.0)
        bits = pltpu.bitcast(s, I32)
        key = jnp.where(bits < 0, bits ^ jnp.int32(0x7FFFFFFF), bits)
        pos = t * tk + lax.broadcasted_iota(I32, (tq, tk), 1)
        key_scr[t] = jnp.where(pos < limit, key, jnp.int32(INT_MIN))
        return carry

    lax.fori_loop(0, nt, score_tile, 0)

    def count_ge(cand):
        def body(t, acc):
            m = jnp.where(key_scr[t] >= cand, 1.0, 0.0)
            for c in range(tk // LANES):
                acc = acc + m[:, c * LANES:(c + 1) * LANES]
            return acc
        acc = lax.fori_loop(0, nt, body, jnp.zeros((tq, LANES), F32))
        return jnp.sum(acc, axis=1, keepdims=True)

    kf = float(k_sel)
    zero = jnp.zeros((tq, 1), I32)
    res = jnp.where(count_ge(zero) >= kf, zero, jnp.int32(INT_MIN))

    def bit_step(i, res):
        cand = res | lax.shift_left(jnp.int32(1), jnp.int32(30) - i)
        return jnp.where(count_ge(cand) >= kf, cand, res)

    res = lax.fori_loop(0, 31, bit_step, res)
    thr = jnp.maximum(res, jnp.int32(INT_MIN + 1))

    rows = NH * tq
    m_scr[...] = jnp.full((rows, 1), M_INIT, F32)
    l_scr[...] = jnp.zeros((rows, 1), F32)
    acc_scr[...] = jnp.zeros((rows, HD), F32)
    gr = QPK * tq

    def attend_tile(t, carry):
        off = pl.multiple_of(t * tk, tk)
        bias = jnp.where(key_scr[t] >= thr, 0.0, NEG_BIAS)
        bias = jnp.concatenate([bias] * QPK, axis=0)
        for g in range(NKV):
            r0 = g * gr
            qg = q_ref[r0:r0 + gr, :]
            kg = k_ref[g, pl.ds(off, tk), :]
            vg = v_ref[g, pl.ds(off, tk), :]
            lg = lax.dot_general(qg, kg, _NT, preferred_element_type=F32) * scale + bias
            m_old = m_scr[r0:r0 + gr, :]
            m_new = jnp.maximum(m_old, jnp.max(lg, axis=1, keepdims=True))
            p = jnp.exp(lg - m_new)
            a = jnp.exp(m_old - m_new)
            l_scr[r0:r0 + gr, :] = a * l_scr[r0:r0 + gr, :] + jnp.sum(p, axis=1, keepdims=True)
            acc_scr[r0:r0 + gr, :] = a * acc_scr[r0:r0 + gr, :] + jnp.dot(
                p.astype(BF), vg, preferred_element_type=F32)
            m_scr[r0:r0 + gr, :] = m_new
        return carry

    lax.fori_loop(0, nt, attend_tile, 0)
    o_ref[...] = (acc_scr[...] / l_scr[...]).astype(BF)


def _attention(limits, qi_r, wi, q_r, kit, k_r, v_r, *, tq, tk, k_sel):
    b, nblk = qi_r.shape[0], qi_r.shape[1]
    s_pad = k_r.shape[2]
    grid_spec = pltpu.PrefetchScalarGridSpec(
        num_scalar_prefetch=1,
        grid=(b, nblk),
        in_specs=[pl.BlockSpec((None, None, NIH * tq, IDIM), lambda i, j, lim: (i, j, 0, 0)),
                  pl.BlockSpec((None, tq, NIH), lambda i, j, lim: (i, j, 0)),
                  pl.BlockSpec((None, None, NH * tq, HD), lambda i, j, lim: (i, j, 0, 0)),
                  pl.BlockSpec((None, s_pad // tk, IDIM, tk), lambda i, j, lim: (i, 0, 0, 0)),
                  pl.BlockSpec((None, NKV, s_pad, HD), lambda i, j, lim: (i, 0, 0, 0)),
                  pl.BlockSpec((None, NKV, s_pad, HD), lambda i, j, lim: (i, 0, 0, 0))],
        out_specs=pl.BlockSpec((None, None, NH * tq, HD), lambda i, j, lim: (i, j, 0, 0)),
        scratch_shapes=[pltpu.VMEM((s_pad // tk, tq, tk), I32),
                        pltpu.VMEM((NIH, tq, tk), F32),
                        pltpu.VMEM((NH * tq, 1), F32),
                        pltpu.VMEM((NH * tq, 1), F32),
                        pltpu.VMEM((NH * tq, HD), F32)])
    return pl.pallas_call(
        functools.partial(_attn_kernel, tq=tq, tk=tk, k_sel=k_sel),
        grid_spec=grid_spec,
        out_shape=jax.ShapeDtypeStruct((b, nblk, NH * tq, HD), BF),
        compiler_params=_params(("arbitrary", "arbitrary")),
        name="attention",
    )(limits, qi_r, wi, q_r, kit, k_r, v_r)


def _post_kernel(x_ref, sc_ref, sh_ref, g1_ref, ya_ref, o_ref, wg_ref, wao_ref, wo_ref, lg_ref, lb_ref, out_ref):
    x = x_ref[...]
    h = (x * (1.0 + sc_ref[...]) + sh_ref[...]).astype(BF)
    gates = jnp.dot(h, wg_ref[...], preferred_element_type=F32)
    ga = jax.nn.sigmoid(gates[:, :D])
    gb = jax.nn.sigmoid(gates[:, D:])
    yb = jnp.dot(o_ref[...], wao_ref[...], preferred_element_type=F32)
    merged = ga * ya_ref[...] + gb * yb
    r = jnp.dot(merged.astype(BF), wo_ref[...], preferred_element_type=F32)
    out_ref[...] = _layer_norm(ALPHA * x + g1_ref[...] * r, lg_ref[...], lb_ref[...])


def _post(x, mod4, ya, o, wg_bf, wao_bf, wo_bf, ln_g, ln_b, *, tm):
    b, t, _ = x.shape
    tile = pl.BlockSpec((None, tm, D), lambda i, j: (i, j, 0))
    full = lambda shape: pl.BlockSpec(shape, lambda i, j: (0,) * len(shape))
    return pl.pallas_call(
        _post_kernel,
        grid=(b, t // tm),
        in_specs=[tile, _mod_spec(1, 2), _mod_spec(0, 2), _mod_spec(2, 2), tile, tile,
                  full((D, 2 * D)), full((NH * HD, D)), full((D, D)), full((1, D)), full((1, D))],
        out_specs=tile,
        out_shape=jax.ShapeDtypeStruct((b, t, D), F32),
        compiler_params=_params(("arbitrary", "arbitrary")),
        name="post",
    )(x, mod4, mod4, mod4, ya, o, wg_bf, wao_bf, wo_bf, ln_g.reshape(1, D), ln_b.reshape(1, D))


def _cmp_exchange(v, i, j, descending):
    hi = jnp.maximum(v[i], v[j])
    lo = jnp.minimum(v[i], v[j])
    v[i], v[j] = (hi, lo) if descending else (lo, hi)


def _bitonic_merge(v):
    n = len(v)
    j = n // 2
    while j >= 1:
        for i in range(n):
            l = i ^ j
            if l > i:
                _cmp_exchange(v, i, l, True)
        j //= 2
    return v


def _bitonic_sort(v):
    n = len(v)
    k = 2
    while k <= n:
        j = k // 2
        while j >= 1:
            for i in range(n):
                l = i ^ j
                if l > i:
                    _cmp_exchange(v, i, l, (i & k) == 0)
            j //= 2
        k *= 2
    return v


def _top16(slabs):
    a = _bitonic_sort(list(slabs))
    for shift in (4, 2, 1):
        c = [jnp.maximum(a[i], pltpu.roll(a[PTOP - 1 - i], shift, axis=0)) for i in range(PTOP)]
        a = _bitonic_merge(c)
    return a


def _peer_prep_kernel(x_ref, sc_ref, sh_ref, wq_ref, gain_ref, k1_ref, k2_ref,
                      h2t_ref, s2_ref, e2_ref, t_ref, c_ref, *, tp):
    nbt = x_ref.shape[0]
    h2 = (x_ref[...] * (1.0 + sc_ref[...]) + sh_ref[...]).reshape(tp, D)
    h2t = h2.T.astype(BF)
    h2t_ref[...] = h2t
    qt = jnp.dot(wq_ref[...], h2t, preferred_element_type=F32)
    sub = lax.broadcasted_iota(I32, (SUBLANES, tp), 0)
    ninf = jnp.float32(-jnp.inf)
    nrep = tp // LANES
    for h in range(PH):
        qh = qt[h * PQ:(h + 1) * PQ, :]
        ms = jnp.mean(qh * qh, axis=0, keepdims=True)
        gain = jnp.concatenate([gain_ref[h * PQ:(h + 1) * PQ, :]] * nrep, axis=1)
        qn = (qh * lax.rsqrt(ms + RMS_EPS) * gain).astype(BF)
        s1 = jnp.dot(k1_ref[h], qn[:PHALF], preferred_element_type=F32).reshape(PTOP, SUBLANES, tp)
        s2 = jnp.dot(k2_ref[h], qn[PHALF:], preferred_element_type=F32).reshape(PTOP, SUBLANES, tp)
        v1 = _top16([s1[i] for i in range(PTOP)])
        v2 = _top16([s2[i] for i in range(PTOP)])
        v1s = v1[SUBLANES - 1]
        for a in range(SUBLANES - 2, -1, -1):
            v1s = jnp.where(sub == a, v1[a], v1s)
        cand = []
        for b in range(PTOP):
            if b < SUBLANES:
                cand.append(jnp.where(sub < PTOP // (b + 1), v1s + v2[b], ninf))
            else:
                cand.append(jnp.where(sub == 0, v1[0] + v2[b], jnp.where(sub == 1, v1[b] + v2[0], ninf)))
        cs = _top16(cand)
        thr = cs[PTOP - 1]
        mx = cs[0]
        z = jnp.exp(cs[0] - mx)
        for k in range(1, PTOP):
            z = z + jnp.exp(cs[k] - mx)
        inv_z = 1.0 / z
        tcut = jnp.full((PTOP, SUBLANES, tp), jnp.inf, F32)
        for b in range(PTOP):
            tcut = jnp.minimum(tcut, jnp.where(s1 + v2[b][None] >= thr[None], v2[b][None], jnp.inf))
        s2_ref[h] = s2.reshape(NKEYS, tp)
        e2_ref[h] = jnp.exp(s2 - v2[0][None]).reshape(NKEYS, tp)
        t_ref[h] = tcut.reshape(NKEYS, tp)
        c_ref[h] = (jnp.exp(s1 - v1[0][None]) * inv_z[None]).reshape(NKEYS, tp)
    del nbt


def _peer_prep(x1, mod4, wqt_bf, gain_b, k1_bf, k2_bf, *, nbt, tt):
    b, t, _ = x1.shape
    tp = nbt * tt
    nb_tiles, nt_tiles = b // nbt, t // tt
    ntile = nb_tiles * nt_tiles
    tidx = lambda i, j: i * nt_tiles + j
    big = lambda: pl.BlockSpec((None, PH, NKEYS, tp), lambda i, j: (tidx(i, j), 0, 0, 0))
    mod = lambda k: pl.BlockSpec((nbt, None, 1, D), lambda i, j: (i, k, 0, 0))
    return pl.pallas_call(
        functools.partial(_peer_prep_kernel, tp=tp),
        grid=(nb_tiles, nt_tiles),
        in_specs=[pl.BlockSpec((nbt, tt, D), lambda i, j: (i, j, 0)),
                  mod(4), mod(3),
                  pl.BlockSpec((PH * PQ, D), lambda i, j: (0, 0)),
                  pl.BlockSpec((PH * PQ, LANES), lambda i, j: (0, 0)),
                  pl.BlockSpec((PH, NKEYS, PHALF), lambda i, j: (0, 0, 0)),
                  pl.BlockSpec((PH, NKEYS, PHALF), lambda i, j: (0, 0, 0))],
        out_specs=[pl.BlockSpec((None, D, tp), lambda i, j: (tidx(i, j), 0, 0)),
                   big(), big(), big(), big()],
        out_shape=[jax.ShapeDtypeStruct((ntile, D, tp), BF)]
        + [jax.ShapeDtypeStruct((ntile, PH, NKEYS, tp), F32)] * 4,
        compiler_params=_params(("arbitrary", "arbitrary")),
        name="peer_prep",
    )(x1, mod4, mod4, wqt_bf, gain_b, k1_bf, k2_bf)


def _peer_main_kernel(h2t_ref, s2_ref, e2_ref, t_ref, c_ref, u_ref, vt_ref, x_ref, g2_ref, lg_ref, lb_ref,
                      out_ref, acc_scr, w_scr, *, ic, tp):
    e = pl.program_id(2)
    ne = pl.num_programs(2)

    @pl.when(e == 0)
    def _():
        acc_scr[...] = jnp.zeros_like(acc_scr)

    h2t = h2t_ref[...]

    def one_first_key(il, carry):
        i = e * ic + il
        r0 = pl.multiple_of(il * NKEYS, NKEYS)
        a = jnp.dot(u_ref[pl.ds(r0, NKEYS), :], h2t, preferred_element_type=F32)
        gate = jnp.zeros((NKEYS, tp), F32)
        for h in range(PH):
            sel = jnp.where(s2_ref[h] >= t_ref[h, pl.ds(i, 1), :], e2_ref[h], 0.0)
            gate = gate + sel * c_ref[h, pl.ds(i, 1), :]
        w_scr[pl.ds(r0, NKEYS), :] = (jax.nn.gelu(a, approximate=True) * gate).astype(BF)
        return carry

    lax.fori_loop(0, ic, one_first_key, 0)
    acc_scr[...] += jnp.dot(vt_ref[...], w_scr[...], preferred_element_type=F32)

    @pl.when(e == ne - 1)
    def _():
        ff = acc_scr[...].T.reshape(out_ref.shape)
        out_ref[...] = _layer_norm(ALPHA * x_ref[...] + g2_ref[...] * ff, lg_ref[...], lb_ref[...])


def _peer_main(h2t, s2, e2, tc, c, u_bf, vt_bf, x1, mod4, ln_g, ln_b, *, nbt, tt, ic):
    b, t, _ = x1.shape
    tp = nbt * tt
    nb_tiles, nt_tiles = b // nbt, t // tt
    tidx = lambda i, j: i * nt_tiles + j
    big = lambda: pl.BlockSpec((None, PH, NKEYS, tp), lambda i, j, e: (tidx(i, j), 0, 0, 0))
    xt = pl.BlockSpec((nbt, tt, D), lambda i, j, e: (i, j, 0))
    return pl.pallas_call(
        functools.partial(_peer_main_kernel, ic=ic, tp=tp),
        grid=(nb_tiles, nt_tiles, NKEYS // ic),
        in_specs=[pl.BlockSpec((None, D, tp), lambda i, j, e: (tidx(i, j), 0, 0)),
                  big(), big(), big(), big(),
                  pl.BlockSpec((ic * NKEYS, D), lambda i, j, e: (e, 0)),
                  pl.BlockSpec((D, ic * NKEYS), lambda i, j, e: (0, e)),
                  xt,
                  pl.BlockSpec((nbt, None, 1, D), lambda i, j, e: (i, 5, 0, 0)),
                  pl.BlockSpec((1, 1, D), lambda i, j, e: (0, 0, 0)),
                  pl.BlockSpec((1, 1, D), lambda i, j, e: (0, 0, 0))],
        out_specs=xt,
        out_shape=jax.ShapeDtypeStruct((b, t, D), F32),
        scratch_shapes=[pltpu.VMEM((D, tp), F32), pltpu.VMEM((ic * NKEYS, tp), BF)],
        compiler_params=_params(("arbitrary", "arbitrary", "arbitrary")),
        name="peer_main",
    )(h2t, s2, e2, tc, c, u_bf, vt_bf, x1, mod4, ln_g.reshape(1, 1, D), ln_b.reshape(1, 1, D))


def _rows_by_head(a, b, nblk, tq, nheads, width):
    a = a.reshape(b, nblk, tq, nheads, width)
    return jnp.transpose(a, (0, 1, 3, 2, 4)).reshape(b, nblk, nheads * tq, width)


def _layer(x, mod, hist, past, w, *, tm, tq, tk, peer_nbt, peer_tt, peer_ic):
    b, t, _ = x.shape
    mod4 = mod.reshape(b, 6, 1, D)
    ya, conv_state = _conv_mixer(x, mod4, hist, w["w3"], w["conv_w"], w["wco"], tm=tm)
    q, k, v, kb, vb, qi, ki, kib, wi = _projections(x, mod4, w["wp"], tm=tm)

    if past is None:
        k_all, v_all, ki_all = kb, vb, kib
        s_len = t
        limits = (jnp.arange(t // tq, dtype=I32) * tq // CHUNK + 1) * CHUNK
        k_sel = min(TOPK_MAX, t // 4)
    else:
        pk, pv, pki = past
        k_all = jnp.concatenate([pk.reshape(b, -1, NKV * HD).astype(BF), kb], axis=1)
        v_all = jnp.concatenate([pv.reshape(b, -1, NKV * HD).astype(BF), vb], axis=1)
        ki_all = jnp.concatenate([pki.astype(BF), kib], axis=1)
        s_len = k_all.shape[1]
        limits = jnp.full((t // tq,), s_len, I32)
        k_sel = min(TOPK_MAX, s_len // 4)
    s_pad = -(-s_len // tk) * tk
    pad = ((0, 0), (0, s_pad - s_len), (0, 0))
    k_all, v_all, ki_all = (jnp.pad(a, pad) for a in (k_all, v_all, ki_all))
    nblk = t // tq
    k_r = jnp.transpose(k_all.reshape(b, s_pad, NKV, HD), (0, 2, 1, 3))
    v_r = jnp.transpose(v_all.reshape(b, s_pad, NKV, HD), (0, 2, 1, 3))
    kit = jnp.transpose(ki_all.reshape(b, s_pad // tk, tk, IDIM), (0, 1, 3, 2))
    qi_r = _rows_by_head(qi, b, nblk, tq, NIH, IDIM)
    q_r = _rows_by_head(q, b, nblk, tq, NH, HD)
    o_r = _attention(limits, qi_r, wi, q_r, kit, k_r, v_r, tq=tq, tk=tk, k_sel=k_sel)
    o = jnp.transpose(o_r.reshape(b, nblk, NH, tq, HD), (0, 1, 3, 2, 4)).reshape(b, t, NH * HD)

    x1 = _post(x, mod4, ya, o, w["wg"], w["wao"], w["wo"], w["ln1_g"], w["ln1_b"], tm=tm)

    h2t, s2, e2, tc, c = _peer_prep(x1, mod4, w["wqt"], w["gain_b"], w["k1"], w["k2"], nbt=peer_nbt, tt=peer_tt)
    y = _peer_main(h2t, s2, e2, tc, c, w["u"], w["vt"], x1, mod4, w["ln2_g"], w["ln2_b"],
                   nbt=peer_nbt, tt=peer_tt, ic=peer_ic)
    return y, k.reshape(b, t, NKV, HD), v.reshape(b, t, NKV, HD), ki, conv_state


def _prep_weights(w_mix_in, conv_w, w_conv_out, w_attn_out, w_o, ln1_g, ln1_b, w_peer_q, peer_q_gain,
                  sub_keys_1, sub_keys_2, expert_u, expert_v, ln2_g, ln2_b):
    c0 = 3 * DC
    wq = w_mix_in[:, c0:c0 + 2 * _QW + 2 * _KW]
    c1 = c0 + 2 * _QW + 2 * _KW
    wki = jnp.pad(w_mix_in[:, c1:c1 + IDIM], ((0, 0), (0, LANES - IDIM)))
    wwi = jnp.pad(w_mix_in[:, c1 + IDIM:c1 + IDIM + NIH], ((0, 0), (0, LANES - NIH)))
    c2 = c1 + IDIM + NIH
    return dict(
        w3=w_mix_in[:, :c0].astype(BF),
        wp=jnp.concatenate([wq, wki, wwi], axis=1).astype(BF),
        wg=w_mix_in[:, c2:c2 + 2 * D].astype(BF),
        conv_w=conv_w, wco=w_conv_out.astype(BF), wao=w_attn_out.astype(BF), wo=w_o.astype(BF),
        ln1_g=ln1_g, ln1_b=ln1_b,
        wqt=w_peer_q.T.astype(BF),
        gain_b=jnp.broadcast_to(peer_q_gain.reshape(PH * PQ, 1), (PH * PQ, LANES)),
        k1=sub_keys_1.astype(BF), k2=sub_keys_2.astype(BF),
        u=expert_u.astype(BF), vt=expert_v.T.astype(BF),
        ln2_g=ln2_g, ln2_b=ln2_b)


def kernel(x_prompt, x_sample, c_prompt, c_sample, cache_k, cache_v, cache_idx_k, state_conv, w_ada, b_ada,
           w_mix_in, conv_w, w_conv_out, w_attn_out, w_o, ln1_g, ln1_b, w_peer_q, peer_q_gain, sub_keys_1,
           sub_keys_2, expert_u, expert_v, ln2_g, ln2_b):
    bp, bs = x_prompt.shape[0], x_sample.shape[0]
    ts = x_sample.shape[1]
    mod = _modulation(jnp.concatenate([c_prompt, c_sample], axis=0), w_ada[0].astype(BF), b_ada[0])
    w = _prep_weights(w_mix_in[0], conv_w[0], w_conv_out[0], w_attn_out[0], w_o[0], ln1_g[0], ln1_b[0],
                      w_peer_q[0], peer_q_gain[0], sub_keys_1[0], sub_keys_2[0], expert_u[0], expert_v[0],
                      ln2_g[0], ln2_b[0])
    zero_hist = jnp.zeros((bp, 2, DC), F32)
    tmp = min(512, x_prompt.shape[1])
    yp, kp, vp, kip, cp = _layer(x_prompt, mod[:bp], zero_hist, None, w,
                                 tm=tmp, tq=CHUNK, tk=512, peer_nbt=1, peer_tt=tmp, peer_ic=8)
    ys, ks, vs, kis, cs = _layer(x_sample, mod[bp:], state_conv[0], (cache_k[0], cache_v[0], cache_idx_k[0]), w,
                                 tm=ts, tq=ts, tk=512, peer_nbt=bs, peer_tt=ts, peer_ic=8)
    return (yp, ys, kp[None], vp[None], kip[None], cp[None], ks[None], vs[None], kis[None], cs[None])
```

```python
import functools

import jax
import jax.numpy as jnp
from jax import lax
from jax.experimental import pallas as pl
from jax.experimental.pallas import tpu as pltpu

BF = jnp.bfloat16
F32 = jnp.float32
I32 = jnp.int32

D = 1024
DC = 1024
CHUNK = 64
NH = 8
NKV = 2
QPK = NH // NKV
HD = 128
NIH = 16
IDIM = 64
TOPK_MAX = 256
PH = 8
PQ = 256
PHALF = PQ // 2
NKEYS = 128
NEXP = NKEYS * NKEYS
PTOP = 16
LN_EPS = 1e-5
RMS_EPS = 1e-6
ALPHA = 2.0 ** 0.25

LANES = 128
SUBLANES = 8
VMEM_LIMIT = 56 * 1024 * 1024
INT_MIN = -2 ** 31
NEG_BIAS = -3e30
M_INIT = -1e30
LOG2E = 1.4426950408889634

_NT = (((1,), (1,)), ((), ()))


def _params(sem):
    return pltpu.CompilerParams(dimension_semantics=sem, vmem_limit_bytes=VMEM_LIMIT)


def _layer_norm(y, g, b):
    mu = jnp.mean(y, axis=-1, keepdims=True)
    yc = y - mu
    var = jnp.mean(yc * yc, axis=-1, keepdims=True)
    return yc * lax.rsqrt(var + LN_EPS) * g + b


def _mod_kernel(c_ref, w_ref, b_ref, o_ref):
    c = c_ref[...]
    s = c * jax.nn.sigmoid(c)
    o_ref[...] = jnp.dot(s.astype(BF), w_ref[...], preferred_element_type=F32) + b_ref[...]


def _modulation(c, w_ada_bf, b_ada):
    nb = c.shape[0]
    n_out = w_ada_bf.shape[1]
    tn = D
    return pl.pallas_call(
        _mod_kernel,
        grid=(n_out // tn,),
        in_specs=[pl.BlockSpec((nb, D), lambda j: (0, 0)),
                  pl.BlockSpec((D, tn), lambda j: (0, j)),
                  pl.BlockSpec((1, tn), lambda j: (0, j))],
        out_specs=pl.BlockSpec((nb, tn), lambda j: (0, j)),
        out_shape=jax.ShapeDtypeStruct((nb, n_out), F32),
        compiler_params=_params(("arbitrary",)),
        name="modulation",
    )(c, w_ada_bf, b_ada.reshape(1, n_out))


def _mod_spec(k, grid_rank):
    if grid_rank == 2:
        return pl.BlockSpec((None, None, 1, D), lambda b, t: (b, k, 0, 0))
    raise ValueError(grid_rank)


def _conv_kernel(x_ref, sc_ref, sh_ref, hist_ref, w3_ref, cw_ref, wco_ref, ya_ref, cs_ref, ubuf, *, tm):
    t = pl.program_id(1)
    h = (x_ref[...] * (1.0 + sc_ref[...]) + sh_ref[...]).astype(BF)
    z = jnp.dot(h, w3_ref[...], preferred_element_type=F32)
    xin = z[:, :DC]
    gb = z[:, DC:2 * DC]
    gc = z[:, 2 * DC:]

    @pl.when(t == 0)
    def _():
        ubuf[0:SUBLANES, :] = jnp.zeros((SUBLANES, DC), F32)
        ubuf[SUBLANES - 2:SUBLANES, :] = hist_ref[...]

    ubuf[SUBLANES:SUBLANES + tm, :] = gc * xin
    cw = cw_ref[...]
    y = (cw[0:1] * ubuf[SUBLANES - 2:SUBLANES - 2 + tm, :]
         + cw[1:2] * ubuf[SUBLANES - 1:SUBLANES - 1 + tm, :]
         + cw[2:3] * ubuf[SUBLANES:SUBLANES + tm, :])
    ya_ref[...] = jnp.dot((gb * y).astype(BF), wco_ref[...], preferred_element_type=F32)
    tail = ubuf[tm:tm + SUBLANES, :]
    ubuf[0:SUBLANES, :] = tail
    cs_ref[...] = tail[SUBLANES - 2:SUBLANES]


def _conv_mixer(x, mod4, hist, w3_bf, conv_w, wco_bf, *, tm):
    b, t, _ = x.shape
    return pl.pallas_call(
        functools.partial(_conv_kernel, tm=tm),
        grid=(b, t // tm),
        in_specs=[pl.BlockSpec((None, tm, D), lambda i, j: (i, j, 0)),
                  _mod_spec(1, 2), _mod_spec(0, 2),
                  pl.BlockSpec((None, 2, DC), lambda i, j: (i, 0, 0)),
                  pl.BlockSpec((D, 3 * DC), lambda i, j: (0, 0)),
                  pl.BlockSpec((3, DC), lambda i, j: (0, 0)),
                  pl.BlockSpec((DC, D), lambda i, j: (0, 0))],
        out_specs=[pl.BlockSpec((None, tm, D), lambda i, j: (i, j, 0)),
                   pl.BlockSpec((None, 2, DC), lambda i, j: (i, 0, 0))],
        out_shape=[jax.ShapeDtypeStruct((b, t, D), F32),
                   jax.ShapeDtypeStruct((b, 2, DC), F32)],
        scratch_shapes=[pltpu.VMEM((tm + SUBLANES, DC), F32)],
        compiler_params=_params(("arbitrary", "arbitrary")),
        name="conv_mixer",
    )(x, mod4, mod4, hist, w3_bf, conv_w, wco_bf)


_QW = NH * HD
_KW = NKV * HD
_PROJ_COLS = (0, _QW, _QW + _KW, _QW + 2 * _KW, 2 * _QW + 2 * _KW,
              2 * _QW + 2 * _KW + LANES, 2 * _QW + 2 * _KW + 2 * LANES)


def _proj_kernel(x_ref, sc_ref, sh_ref, w_ref, q_ref, k_ref, v_ref, kb_ref, vb_ref, qi_ref, ki_ref, kib_ref, wi_ref):
    h = (x_ref[...] * (1.0 + sc_ref[...]) + sh_ref[...]).astype(BF)
    z = jnp.dot(h, w_ref[...], preferred_element_type=F32)
    c = _PROJ_COLS
    q_ref[...] = z[:, c[0]:c[1]].astype(BF)
    k = z[:, c[1]:c[2]]
    v = z[:, c[2]:c[3]]
    k_ref[...] = k
    v_ref[...] = v
    kb_ref[...] = k.astype(BF)
    vb_ref[...] = v.astype(BF)
    qi_ref[...] = z[:, c[3]:c[4]].astype(BF)
    ki = z[:, c[4]:c[4] + IDIM]
    ki_ref[...] = ki
    kib_ref[...] = ki.astype(BF)
    wi_ref[...] = z[:, c[5]:c[5] + NIH]


def _projections(x, mod4, wp_bf, *, tm):
    b, t, _ = x.shape
    widths = (_QW, _KW, _KW, _KW, _KW, NIH * IDIM, IDIM, IDIM, NIH)
    dtypes = (BF, F32, F32, BF, BF, BF, F32, BF, F32)
    return pl.pallas_call(
        _proj_kernel,
        grid=(b, t // tm),
        in_specs=[pl.BlockSpec((None, tm, D), lambda i, j: (i, j, 0)),
                  _mod_spec(1, 2), _mod_spec(0, 2),
                  pl.BlockSpec(wp_bf.shape, lambda i, j: (0, 0))],
        out_specs=[pl.BlockSpec((None, tm, w), lambda i, j: (i, j, 0)) for w in widths],
        out_shape=[jax.ShapeDtypeStruct((b, t, w), dt) for w, dt in zip(widths, dtypes)],
        compiler_params=_params(("arbitrary", "arbitrary")),
        name="projections",
    )(x, mod4, mod4, wp_bf)


def _attn_kernel(lim_ref, qi_ref, wi_ref, q_ref, kit_ref, k_ref, v_ref, o_ref,
                 key_scr, wb_scr, x_scr, bias_scr, lg_scr, p_scr, m_scr, l_scr, acc_scr, *, tq, tk, k_sel):
    n = pl.program_id(1)
    limit = lim_ref[n]
    nt = lax.div(limit + (tk - 1), tk)
    chunk_back = (tq - 1) // CHUNK - lax.broadcasted_iota(I32, (tq, 1), 0) // CHUNK
    lim_rows = limit - CHUNK * chunk_back

    wi = wi_ref[...]
    for h in range(NIH):
        wb_scr[h] = jnp.broadcast_to(wi[:, h:h + 1], (tq, LANES))
    qi = qi_ref[...]
    nlt = tk // LANES
    lane_iota = lax.broadcasted_iota(I32, (tq, LANES), 1)

    def score_tile(t, carry):
        x_scr[...] = jnp.dot(qi, kit_ref[t], preferred_element_type=F32)
        for c in range(nlt):
            cs = slice(c * LANES, (c + 1) * LANES)
            s = wb_scr[0] * jnp.maximum(x_scr[0:tq, cs], 0.0)
            for h in range(1, NIH):
                s = s + wb_scr[h] * jnp.maximum(x_scr[h * tq:(h + 1) * tq, cs], 0.0)
            bits = pltpu.bitcast(s, I32)
            key = jnp.where(bits < 0, bits ^ jnp.int32(0x7FFFFFFF), bits)
            pos = t * tk + c * LANES + lane_iota
            key_scr[t, :, cs] = jnp.where(pos < lim_rows, key, jnp.int32(INT_MIN))
        return carry

    lax.fori_loop(0, nt, score_tile, 0)

    def count_ge(cand):
        def body(t, acc):
            m = jnp.where(key_scr[t] >= cand, 1.0, 0.0)
            for c in range(tk // LANES):
                acc = acc + m[:, c * LANES:(c + 1) * LANES]
            return acc
        acc = lax.fori_loop(0, nt, body, jnp.zeros((tq, LANES), F32))
        return jnp.sum(acc, axis=1, keepdims=True)

    kf = float(k_sel)
    zero = jnp.zeros((tq, 1), I32)
    res = jnp.where(count_ge(zero) >= kf, zero, jnp.int32(INT_MIN))

    def bit_step(i, res):
        cand = res | lax.shift_left(jnp.int32(1), jnp.int32(30) - i)
        return jnp.where(count_ge(cand) >= kf, cand, res)

    res = lax.fori_loop(0, 31, bit_step, res)
    thr = jnp.maximum(res, jnp.int32(INT_MIN + 1))

    rows = NH * tq
    m_scr[...] = jnp.full((rows, 1), M_INIT, F32)
    l_scr[...] = jnp.zeros((rows, LANES), F32)
    acc_scr[...] = jnp.zeros((rows, HD), F32)
    gr = QPK * tq
    c1 = (HD ** -0.5) * LOG2E
    rc = min(4 * SUBLANES, tq)

    def lane_tiles(x):
        return [x[:, c * LANES:(c + 1) * LANES] for c in range(nlt)]

    def attend_tile(t, carry):
        off = pl.multiple_of(t * tk, tk)
        bias_scr[...] = jnp.where(key_scr[t] >= thr, 0.0, NEG_BIAS)
        for g in range(NKV):
            kg = k_ref[g, pl.ds(off, tk), :]
            lg_scr[g] = lax.dot_general(q_ref[g * gr:(g + 1) * gr, :], kg, _NT, preferred_element_type=F32)
            for sc in range(gr // rc):
                r0 = sc * rc
                b0 = r0 % tq
                rows = slice(g * gr + r0, g * gr + r0 + rc)
                x = lg_scr[g, r0:r0 + rc, :] * c1 + bias_scr[b0:b0 + rc, :]
                m_old = m_scr[rows, :]
                m_new = jnp.maximum(m_old, jnp.max(functools.reduce(jnp.maximum, lane_tiles(x)),
                                                   axis=1, keepdims=True))
                p = jnp.exp2(x - m_new)
                a = jnp.exp2(m_old - m_new)
                l_scr[rows, :] = a * l_scr[rows, :] + functools.reduce(jnp.add, lane_tiles(p))
                m_scr[rows, :] = m_new
                acc_scr[rows, :] = a * acc_scr[rows, :]
                p_scr[g, r0:r0 + rc, :] = p.astype(BF)
            acc_scr[g * gr:(g + 1) * gr, :] += jnp.dot(p_scr[g], v_ref[g, pl.ds(off, tk), :],
                                                       preferred_element_type=F32)
        return carry

    lax.fori_loop(0, nt, attend_tile, 0)
    l_tot = jnp.sum(l_scr[...], axis=1, keepdims=True)
    o_ref[...] = (acc_scr[...] / l_tot).astype(BF)


def _attention(limits, qi_r, wi, q_r, kit, k_r, v_r, *, tq, tk, k_sel):
    b, nblk = qi_r.shape[0], qi_r.shape[1]
    s_pad = k_r.shape[2]
    grid_spec = pltpu.PrefetchScalarGridSpec(
        num_scalar_prefetch=1,
        grid=(b, nblk),
        in_specs=[pl.BlockSpec((None, None, NIH * tq, IDIM), lambda i, j, lim: (i, j, 0, 0)),
                  pl.BlockSpec((None, tq, NIH), lambda i, j, lim: (i, j, 0)),
                  pl.BlockSpec((None, None, NH * tq, HD), lambda i, j, lim: (i, j, 0, 0)),
                  pl.BlockSpec((None, s_pad // tk, IDIM, tk), lambda i, j, lim: (i, 0, 0, 0)),
                  pl.BlockSpec((None, NKV, s_pad, HD), lambda i, j, lim: (i, 0, 0, 0)),
                  pl.BlockSpec((None, NKV, s_pad, HD), lambda i, j, lim: (i, 0, 0, 0))],
        out_specs=pl.BlockSpec((None, None, NH * tq, HD), lambda i, j, lim: (i, j, 0, 0)),
        scratch_shapes=[pltpu.VMEM((s_pad // tk, tq, tk), I32),
                        pltpu.VMEM((NIH, tq, LANES), F32),
                        pltpu.VMEM((NIH * tq, tk), F32),
                        pltpu.VMEM((tq, tk), F32),
                        pltpu.VMEM((NKV, QPK * tq, tk), F32),
                        pltpu.VMEM((NKV, QPK * tq, tk), BF),
                        pltpu.VMEM((NH * tq, 1), F32),
                        pltpu.VMEM((NH * tq, LANES), F32),
                        pltpu.VMEM((NH * tq, HD), F32)])
    return pl.pallas_call(
        functools.partial(_attn_kernel, tq=tq, tk=tk, k_sel=k_sel),
        grid_spec=grid_spec,
        out_shape=jax.ShapeDtypeStruct((b, nblk, NH * tq, HD), BF),
        compiler_params=_params(("arbitrary", "arbitrary")),
        name="attention",
    )(limits, qi_r, wi, q_r, kit, k_r, v_r)


def _post_kernel(x_ref, sc_ref, sh_ref, g1_ref, ya_ref, o_ref, wg_ref, wao_ref, wo_ref, lg_ref, lb_ref, out_ref):
    x = x_ref[...]
    h = (x * (1.0 + sc_ref[...]) + sh_ref[...]).astype(BF)
    gates = jnp.dot(h, wg_ref[...], preferred_element_type=F32)
    ga = jax.nn.sigmoid(gates[:, :D])
    gb = jax.nn.sigmoid(gates[:, D:])
    yb = jnp.dot(o_ref[...], wao_ref[...], preferred_element_type=F32)
    merged = ga * ya_ref[...] + gb * yb
    r = jnp.dot(merged.astype(BF), wo_ref[...], preferred_element_type=F32)
    out_ref[...] = _layer_norm(ALPHA * x + g1_ref[...] * r, lg_ref[...], lb_ref[...])


def _post(x, mod4, ya, o, wg_bf, wao_bf, wo_bf, ln_g, ln_b, *, tm):
    b, t, _ = x.shape
    tile = pl.BlockSpec((None, tm, D), lambda i, j: (i, j, 0))
    full = lambda shape: pl.BlockSpec(shape, lambda i, j: (0,) * len(shape))
    return pl.pallas_call(
        _post_kernel,
        grid=(b, t // tm),
        in_specs=[tile, _mod_spec(1, 2), _mod_spec(0, 2), _mod_spec(2, 2), tile, tile,
                  full((D, 2 * D)), full((NH * HD, D)), full((D, D)), full((1, D)), full((1, D))],
        out_specs=tile,
        out_shape=jax.ShapeDtypeStruct((b, t, D), F32),
        compiler_params=_params(("arbitrary", "arbitrary")),
        name="post",
    )(x, mod4, mod4, mod4, ya, o, wg_bf, wao_bf, wo_bf, ln_g.reshape(1, D), ln_b.reshape(1, D))


def _cmp_exchange(v, i, j, descending):
    hi = jnp.maximum(v[i], v[j])
    lo = jnp.minimum(v[i], v[j])
    v[i], v[j] = (hi, lo) if descending else (lo, hi)


def _bitonic_merge(v):
    n = len(v)
    j = n // 2
    while j >= 1:
        for i in range(n):
            l = i ^ j
            if l > i:
                _cmp_exchange(v, i, l, True)
        j //= 2
    return v


def _bitonic_sort(v):
    n = len(v)
    k = 2
    while k <= n:
        j = k // 2
        while j >= 1:
            for i in range(n):
                l = i ^ j
                if l > i:
                    _cmp_exchange(v, i, l, (i & k) == 0)
            j //= 2
        k *= 2
    return v


def _top16(slabs):
    a = _bitonic_sort(list(slabs))
    for shift in (4, 2, 1):
        c = [jnp.maximum(a[i], pltpu.roll(a[PTOP - 1 - i], shift, axis=0)) for i in range(PTOP)]
        a = _bitonic_merge(c)
    return a


def _bf16_pair_words(x):
    hi = pltpu.bitcast(x.astype(BF).astype(F32), I32)
    return hi | lax.shift_right_logical(hi, jnp.int32(16))


def _rows_from_words(word_row):
    packed = pltpu.bitcast(jnp.broadcast_to(word_row, (SUBLANES, LANES)), BF)
    return jnp.concatenate([packed] * (NKEYS // (2 * SUBLANES)), axis=0)


def _peer_prep_kernel(x_ref, sc_ref, sh_ref, wq_ref, gain_ref, k1_ref, k2_ref,
                      h2t_ref, r2_ref, e2_ref, nb_ref, c_ref, *, tp):
    h2 =(x_ref[...] * (1.0 + sc_ref[...]) + sh_ref[...]).reshape(tp, D)
    h2t = h2.T.astype(BF)
    h2t_ref[...] = h2t
    qt = jnp.dot(wq_ref[...], h2t, preferred_element_type=F32)
    sub = lax.broadcasted_iota(I32, (SUBLANES, tp), 0)
    ninf = jnp.float32(-jnp.inf)
    nrep = tp // LANES
    for h in range(PH):
        qh = qt[h * PQ:(h + 1) * PQ, :]
        ms = jnp.mean(qh * qh, axis=0, keepdims=True)
        gain = jnp.concatenate([gain_ref[h * PQ:(h + 1) * PQ, :]] * nrep, axis=1)
        qn = (qh * lax.rsqrt(ms + RMS_EPS) * gain).astype(BF)
        s1 = jnp.dot(k1_ref[h], qn[:PHALF], preferred_element_type=F32).reshape(PTOP, SUBLANES, tp)
        s2 = jnp.dot(k2_ref[h], qn[PHALF:], preferred_element_type=F32).reshape(PTOP, SUBLANES, tp)
        v1 = _top16([s1[i] for i in range(PTOP)])
        v2 = _top16([s2[i] for i in range(PTOP)])
        v1s = v1[SUBLANES - 1]
        for a in range(SUBLANES - 2, -1, -1):
            v1s = jnp.where(sub == a, v1[a], v1s)
        cand = []
        for b in range(PTOP):
            if b < SUBLANES:
                cand.append(jnp.where(sub < PTOP // (b + 1), v1s + v2[b], ninf))
            else:
                cand.append(jnp.where(sub == 0, v1[0] + v2[b], jnp.where(sub == 1, v1[b] + v2[0], ninf)))
        cs = _top16(cand)
        thr = cs[PTOP - 1]
        mx = cs[0]
        z = jnp.exp(cs[0] - mx)
        for k in range(1, PTOP):
            z = z + jnp.exp(cs[k] - mx)
        inv_z = 1.0 / z
        nb = jnp.zeros((PTOP, SUBLANES, tp), F32)
        rank2 = jnp.zeros((PTOP, SUBLANES, tp), F32)
        for b in range(PTOP):
            nb = nb + jnp.where(s1 + v2[b][None] >= thr[None], 1.0, 0.0)
            rank2 = rank2 + jnp.where(v2[b][None] > s2, 1.0, 0.0)
        r2_ref[h] = pltpu.bitcast(rank2.reshape(NKEYS, tp).astype(BF), I32)
        e2_ref[h] = pltpu.bitcast(jnp.exp(s2 - v2[0][None]).reshape(NKEYS, tp).astype(BF), I32)
        nb_ref[h] = _bf16_pair_words(nb.reshape(NKEYS, tp))
        c_ref[h] = _bf16_pair_words((jnp.exp(s1 - v1[0][None]) * inv_z[None]).reshape(NKEYS, tp))


def _peer_prep(x1, mod4, wqt_bf, gain_b, k1_bf, k2_bf, *, nbt, tt):
    b, t, _ = x1.shape
    tp = nbt * tt
    nb_tiles, nt_tiles = b // nbt, t // tt
    ntile = nb_tiles * nt_tiles
    tidx = lambda i, j: i * nt_tiles + j
    big = lambda r: pl.BlockSpec((None, PH, r, tp), lambda i, j: (tidx(i, j), 0, 0, 0))
    mod = lambda k: pl.BlockSpec((nbt, None, 1, D), lambda i, j: (i, k, 0, 0))
    return pl.pallas_call(
        functools.partial(_peer_prep_kernel, tp=tp),
        grid=(nb_tiles, nt_tiles),
        in_specs=[pl.BlockSpec((nbt, tt, D), lambda i, j: (i, j, 0)),
                  mod(4), mod(3),
                  pl.BlockSpec((PH * PQ, D), lambda i, j: (0, 0)),
                  pl.BlockSpec((PH * PQ, LANES), lambda i, j: (0, 0)),
                  pl.BlockSpec((PH, NKEYS, PHALF), lambda i, j: (0, 0, 0)),
                  pl.BlockSpec((PH, NKEYS, PHALF), lambda i, j: (0, 0, 0))],
        out_specs=[pl.BlockSpec((None, D, tp), lambda i, j: (tidx(i, j), 0, 0)),
                   big(NKEYS // 2), big(NKEYS // 2), big(NKEYS), big(NKEYS)],
        out_shape=[jax.ShapeDtypeStruct((ntile, D, tp), BF)]
        + [jax.ShapeDtypeStruct((ntile, PH, r, tp), I32) for r in (NKEYS // 2, NKEYS // 2, NKEYS, NKEYS)],
        compiler_params=_params(("arbitrary", "arbitrary")),
        name="peer_prep",
    )(x1, mod4, mod4, wqt_bf, gain_b, k1_bf, k2_bf)


def _gelu_tanh(a):
    k0 = -2.0 * LOG2E * 0.7978845608028654
    k1 = k0 * 0.044715
    return a / (1.0 + jnp.exp2(a * (k0 + k1 * (a * a))))


def _peer_main_kernel(h2t_ref, r2_ref, e2_ref, nb_ref, c_ref, u_ref, vt_ref, x_ref, g2_ref, lg_ref, lb_ref,
                      out_ref, acc_scr, a_scr, w_scr, *, ic, tp):
    e = pl.program_id(2)
    ne = pl.num_programs(2)

    @pl.when(e == 0)
    def _():
        acc_scr[...] = jnp.zeros_like(acc_scr)

    h2t = h2t_ref[...]

    def expert_inputs(il):
        r0 = pl.multiple_of(jnp.minimum(il, ic - 1) * NKEYS, NKEYS)
        return jnp.dot(u_ref[pl.ds(r0, NKEYS), :], h2t, preferred_element_type=F32)

    def gated(il, a_ref):
        i = e * ic + il
        r0 = pl.multiple_of(il * NKEYS, NKEYS)
        nb_rows = [nb_ref[h, pl.ds(i, 1), :] for h in range(PH)]
        c_rows = [c_ref[h, pl.ds(i, 1), :] for h in range(PH)]
        for c in range(tp // LANES):
            cs = slice(c * LANES, (c + 1) * LANES)
            gate = None
            for h in range(PH):
                sel = jnp.where(pltpu.bitcast(r2_ref[h, :, cs], BF) < _rows_from_words(nb_rows[h][:, cs]),
                                pltpu.bitcast(e2_ref[h, :, cs], BF), jnp.zeros((), BF))
                term = sel * _rows_from_words(c_rows[h][:, cs])
                gate = term if gate is None else gate + term
            act = _gelu_tanh(a_ref[:, cs]).astype(BF)
            w_scr[pl.ds(r0, NKEYS), cs] = act * gate

    a_scr[0] = expert_inputs(0)

    def key_pair(ip, carry):
        a_scr[1] = expert_inputs(2 * ip + 1)
        gated(2 * ip, a_scr.at[0])
        a_scr[0] = expert_inputs(2 * ip + 2)
        gated(2 * ip + 1, a_scr.at[1])
        return carry

    lax.fori_loop(0, ic // 2, key_pair, 0)
    acc_scr[...] += jnp.dot(vt_ref[...], w_scr[...], preferred_element_type=F32)

    @pl.when(e == ne - 1)
    def _():
        ff = acc_scr[...].T.reshape(out_ref.shape)
        out_ref[...] = _layer_norm(ALPHA * x_ref[...] + g2_ref[...] * ff, lg_ref[...], lb_ref[...])


def _peer_main(h2t, s2, e2, tc, c, u_bf, vt_bf, x1, mod4, ln_g, ln_b, *, nbt, tt, ic):
    b, t, _ = x1.shape
    tp = nbt * tt
    nb_tiles, nt_tiles = b // nbt, t // tt
    tidx = lambda i, j: i * nt_tiles + j
    big = lambda r: pl.BlockSpec((None, PH, r, tp), lambda i, j, e: (tidx(i, j), 0, 0, 0))
    xt = pl.BlockSpec((nbt, tt, D), lambda i, j, e: (i, j, 0))
    return pl.pallas_call(
        functools.partial(_peer_main_kernel, ic=ic, tp=tp),
        grid=(nb_tiles, nt_tiles, NKEYS // ic),
        in_specs=[pl.BlockSpec((None, D, tp), lambda i, j, e: (tidx(i, j), 0, 0)),
                  big(NKEYS // 2), big(NKEYS // 2), big(NKEYS), big(NKEYS),
                  pl.BlockSpec((ic * NKEYS, D), lambda i, j, e: (e, 0)),
                  pl.BlockSpec((D, ic * NKEYS), lambda i, j, e: (0, e)),
                  xt,
                  pl.BlockSpec((nbt, None, 1, D), lambda i, j, e: (i, 5, 0, 0)),
                  pl.BlockSpec((1, 1, D), lambda i, j, e: (0, 0, 0)),
                  pl.BlockSpec((1, 1, D), lambda i, j, e: (0, 0, 0))],
        out_specs=xt,
        out_shape=jax.ShapeDtypeStruct((b, t, D), F32),
        scratch_shapes=[pltpu.VMEM((D, tp), F32), pltpu.VMEM((2, NKEYS, tp), F32),
                        pltpu.VMEM((ic * NKEYS, tp), BF)],
        compiler_params=_params(("arbitrary", "arbitrary", "arbitrary")),
        name="peer_main",
    )(h2t, s2, e2, tc, c, u_bf, vt_bf, x1, mod4, ln_g.reshape(1, 1, D), ln_b.reshape(1, 1, D))


def _rows_by_head(a, b, nblk, tq, nheads, width):
    a = a.reshape(b, nblk, tq, nheads, width)
    return jnp.transpose(a, (0, 1, 3, 2, 4)).reshape(b, nblk, nheads * tq, width)


def _layer(x, mod, hist, past, w, *, tm, tq, tk, peer_nbt, peer_tt, peer_ic):
    b, t, _ = x.shape
    mod4 = mod.reshape(b, 6, 1, D)
    ya, conv_state = _conv_mixer(x, mod4, hist, w["w3"], w["conv_w"], w["wco"], tm=tm)
    q, k, v, kb, vb, qi, ki, kib, wi = _projections(x, mod4, w["wp"], tm=tm)

    if past is None:
        k_all, v_all, ki_all = kb, vb, kib
        s_len = t
        limits = ((jnp.arange(t // tq, dtype=I32) * tq + (tq - 1)) // CHUNK + 1) * CHUNK
        k_sel = min(TOPK_MAX, t // 4)
    else:
        pk, pv, pki = past
        k_all = jnp.concatenate([pk.reshape(b, -1, NKV * HD).astype(BF), kb], axis=1)
        v_all = jnp.concatenate([pv.reshape(b, -1, NKV * HD).astype(BF), vb], axis=1)
        ki_all = jnp.concatenate([pki.astype(BF), kib], axis=1)
        s_len = k_all.shape[1]
        limits = jnp.full((t // tq,), s_len, I32)
        k_sel = min(TOPK_MAX, s_len // 4)
    s_pad = -(-s_len // tk) * tk
    pad = ((0, 0), (0, s_pad - s_len), (0, 0))
    k_all, v_all, ki_all = (jnp.pad(a, pad) for a in (k_all, v_all, ki_all))
    nblk = t // tq
    k_r = jnp.transpose(k_all.reshape(b, s_pad, NKV, HD), (0, 2, 1, 3))
    v_r = jnp.transpose(v_all.reshape(b, s_pad, NKV, HD), (0, 2, 1, 3))
    kit = jnp.transpose(ki_all.reshape(b, s_pad // tk, tk, IDIM), (0, 1, 3, 2))
    qi_r = _rows_by_head(qi, b, nblk, tq, NIH, IDIM)
    q_r = _rows_by_head(q, b, nblk, tq, NH, HD)
    o_r = _attention(limits, qi_r, wi, q_r, kit, k_r, v_r, tq=tq, tk=tk, k_sel=k_sel)
    o = jnp.transpose(o_r.reshape(b, nblk, NH, tq, HD), (0, 1, 3, 2, 4)).reshape(b, t, NH * HD)

    x1 = _post(x, mod4, ya, o, w["wg"], w["wao"], w["wo"], w["ln1_g"], w["ln1_b"], tm=tm)

    h2t, s2, e2, tc, c = _peer_prep(x1, mod4, w["wqt"], w["gain_b"], w["k1"], w["k2"], nbt=peer_nbt, tt=peer_tt)
    y = _peer_main(h2t, s2, e2, tc, c, w["u"], w["vt"], x1, mod4, w["ln2_g"], w["ln2_b"],
                   nbt=peer_nbt, tt=peer_tt, ic=peer_ic)
    return y, k.reshape(b, t, NKV, HD), v.reshape(b, t, NKV, HD), ki, conv_state


def _prep_weights(w_mix_in, conv_w, w_conv_out, w_attn_out, w_o, ln1_g, ln1_b, w_peer_q, peer_q_gain,
                  sub_keys_1, sub_keys_2, expert_u, expert_v, ln2_g, ln2_b):
    c0 = 3 * DC
    wq = w_mix_in[:, c0:c0 + 2 * _QW + 2 * _KW]
    c1 = c0 + 2 * _QW + 2 * _KW
    wki = jnp.pad(w_mix_in[:, c1:c1 + IDIM], ((0, 0), (0, LANES - IDIM)))
    wwi = jnp.pad(w_mix_in[:, c1 + IDIM:c1 + IDIM + NIH], ((0, 0), (0, LANES - NIH)))
    c2 = c1 + IDIM + NIH
    return dict(
        w3=w_mix_in[:, :c0].astype(BF),
        wp=jnp.concatenate([wq, wki, wwi], axis=1).astype(BF),
        wg=w_mix_in[:, c2:c2 + 2 * D].astype(BF),
        conv_w=conv_w, wco=w_conv_out.astype(BF), wao=w_attn_out.astype(BF), wo=w_o.astype(BF),
        ln1_g=ln1_g, ln1_b=ln1_b,
        wqt=w_peer_q.T.astype(BF),
        gain_b=jnp.broadcast_to(peer_q_gain.reshape(PH * PQ, 1), (PH * PQ, LANES)),
        k1=sub_keys_1.astype(BF), k2=sub_keys_2.astype(BF),
        u=expert_u.astype(BF), vt=expert_v.T.astype(BF),
        ln2_g=ln2_g, ln2_b=ln2_b)


def kernel(x_prompt, x_sample, c_prompt, c_sample, cache_k, cache_v, cache_idx_k, state_conv, w_ada, b_ada,
           w_mix_in, conv_w, w_conv_out, w_attn_out, w_o, ln1_g, ln1_b, w_peer_q, peer_q_gain, sub_keys_1,
           sub_keys_2, expert_u, expert_v, ln2_g, ln2_b):
    bp, bs = x_prompt.shape[0], x_sample.shape[0]
    ts = x_sample.shape[1]
    mod = _modulation(jnp.concatenate([c_prompt, c_sample], axis=0), w_ada[0].astype(BF), b_ada[0])
    w = _prep_weights(w_mix_in[0], conv_w[0], w_conv_out[0], w_attn_out[0], w_o[0], ln1_g[0], ln1_b[0],
                      w_peer_q[0], peer_q_gain[0], sub_keys_1[0], sub_keys_2[0], expert_u[0], expert_v[0],
                      ln2_g[0], ln2_b[0])
    zero_hist = jnp.zeros((bp, 2, DC), F32)
    tmp = min(512, x_prompt.shape[1])
    yp, kp, vp, kip, cp = _layer(x_prompt, mod[:bp], zero_hist, None, w,
                                 tm=tmp, tq=2 * CHUNK, tk=512, peer_nbt=1, peer_tt=tmp, peer_ic=8)
    ys, ks, vs, kis, cs = _layer(x_sample, mod[bp:], state_conv[0], (cache_k[0], cache_v[0], cache_idx_k[0]), w,
                                 tm=ts, tq=ts, tk=512, peer_nbt=bs, peer_tt=ts, peer_ic=8)
    return (yp, ys, kp[None], vp[None], kip[None], cp[None], ks[None], vs[None], kis[None], cs[None])
```

```python
import functools

import jax
import jax.numpy as jnp
from jax import lax
from jax.experimental import pallas as pl
from jax.experimental.pallas import tpu as pltpu

BF = jnp.bfloat16
F32 = jnp.float32
I32 = jnp.int32

D = 1024
DC = 1024
CHUNK = 64
NH = 8
NKV = 2
QPK = NH // NKV
HD = 128
NIH = 16
IDIM = 64
TOPK_MAX = 256
PH = 8
PQ = 256
PHALF = PQ // 2
NKEYS = 128
NEXP = NKEYS * NKEYS
PTOP = 16
LN_EPS = 1e-5
RMS_EPS = 1e-6
ALPHA = 2.0 ** 0.25

LANES = 128
SUBLANES = 8
VMEM_LIMIT = 56 * 1024 * 1024
INT_MIN = -2 ** 31
NEG_BIAS = -3e30
M_INIT = -1e30
LOG2E = 1.4426950408889634

_NT = (((1,), (1,)), ((), ()))


def _params(sem):
    return pltpu.CompilerParams(dimension_semantics=sem, vmem_limit_bytes=VMEM_LIMIT)


def _layer_norm(y, g, b):
    mu = jnp.mean(y, axis=-1, keepdims=True)
    yc = y - mu
    var = jnp.mean(yc * yc, axis=-1, keepdims=True)
    return yc * lax.rsqrt(var + LN_EPS) * g + b


def _mod_kernel(c_ref, w_ref, b_ref, o_ref):
    c = c_ref[...]
    s = c * jax.nn.sigmoid(c)
    o_ref[...] = jnp.dot(s.astype(BF), w_ref[...], preferred_element_type=F32) + b_ref[...]


def _modulation(c, w_ada_bf, b_ada):
    nb = c.shape[0]
    n_out = w_ada_bf.shape[1]
    tn = D
    return pl.pallas_call(
        _mod_kernel,
        grid=(n_out // tn,),
        in_specs=[pl.BlockSpec((nb, D), lambda j: (0, 0)),
                  pl.BlockSpec((D, tn), lambda j: (0, j)),
                  pl.BlockSpec((1, tn), lambda j: (0, j))],
        out_specs=pl.BlockSpec((nb, tn), lambda j: (0, j)),
        out_shape=jax.ShapeDtypeStruct((nb, n_out), F32),
        compiler_params=_params(("arbitrary",)),
        name="modulation",
    )(c, w_ada_bf, b_ada.reshape(1, n_out))


def _mod_spec(k, grid_rank):
    if grid_rank == 2:
        return pl.BlockSpec((None, None, 1, D), lambda b, t: (b, k, 0, 0))
    raise ValueError(grid_rank)


def _conv_kernel(x_ref, sc_ref, sh_ref, hist_ref, w3_ref, cw_ref, wco_ref, ya_ref, cs_ref, ubuf, *, tm):
    t = pl.program_id(1)
    h = (x_ref[...] * (1.0 + sc_ref[...]) + sh_ref[...]).astype(BF)
    z = jnp.dot(h, w3_ref[...], preferred_element_type=F32)
    xin = z[:, :DC]
    gb = z[:, DC:2 * DC]
    gc = z[:, 2 * DC:]

    @pl.when(t == 0)
    def _():
        ubuf[0:SUBLANES, :] = jnp.zeros((SUBLANES, DC), F32)
        ubuf[SUBLANES - 2:SUBLANES, :] = hist_ref[...]

    ubuf[SUBLANES:SUBLANES + tm, :] = gc * xin
    cw = cw_ref[...]
    y = (cw[0:1] * ubuf[SUBLANES - 2:SUBLANES - 2 + tm, :]
         + cw[1:2] * ubuf[SUBLANES - 1:SUBLANES - 1 + tm, :]
         + cw[2:3] * ubuf[SUBLANES:SUBLANES + tm, :])
    ya_ref[...] = jnp.dot((gb * y).astype(BF), wco_ref[...], preferred_element_type=F32)
    tail = ubuf[tm:tm + SUBLANES, :]
    ubuf[0:SUBLANES, :] = tail
    cs_ref[...] = tail[SUBLANES - 2:SUBLANES]


def _conv_mixer(x, mod4, hist, w3_bf, conv_w, wco_bf, *, tm):
    b, t, _ = x.shape
    return pl.pallas_call(
        functools.partial(_conv_kernel, tm=tm),
        grid=(b, t // tm),
        in_specs=[pl.BlockSpec((None, tm, D), lambda i, j: (i, j, 0)),
                  _mod_spec(1, 2), _mod_spec(0, 2),
                  pl.BlockSpec((None, 2, DC), lambda i, j: (i, 0, 0)),
                  pl.BlockSpec((D, 3 * DC), lambda i, j: (0, 0)),
                  pl.BlockSpec((3, DC), lambda i, j: (0, 0)),
                  pl.BlockSpec((DC, D), lambda i, j: (0, 0))],
        out_specs=[pl.BlockSpec((None, tm, D), lambda i, j: (i, j, 0)),
                   pl.BlockSpec((None, 2, DC), lambda i, j: (i, 0, 0))],
        out_shape=[jax.ShapeDtypeStruct((b, t, D), F32),
                   jax.ShapeDtypeStruct((b, 2, DC), F32)],
        scratch_shapes=[pltpu.VMEM((tm + SUBLANES, DC), F32)],
        compiler_params=_params(("arbitrary", "arbitrary")),
        name="conv_mixer",
    )(x, mod4, mod4, hist, w3_bf, conv_w, wco_bf)


_QW = NH * HD
_KW = NKV * HD
_PROJ_COLS = (0, _QW, _QW + _KW, _QW + 2 * _KW, 2 * _QW + 2 * _KW,
              2 * _QW + 2 * _KW + LANES, 2 * _QW + 2 * _KW + 2 * LANES)


def _proj_kernel(x_ref, sc_ref, sh_ref, w_ref, q_ref, k_ref, v_ref, kb_ref, vb_ref, qi_ref, ki_ref, kib_ref, wi_ref):
    h = (x_ref[...] * (1.0 + sc_ref[...]) + sh_ref[...]).astype(BF)
    z = jnp.dot(h, w_ref[...], preferred_element_type=F32)
    c = _PROJ_COLS
    q_ref[...] = z[:, c[0]:c[1]].astype(BF)
    k = z[:, c[1]:c[2]]
    v = z[:, c[2]:c[3]]
    k_ref[...] = k
    v_ref[...] = v
    kb_ref[...] = k.astype(BF)
    vb_ref[...] = v.astype(BF)
    qi_ref[...] = z[:, c[3]:c[4]].astype(BF)
    ki = z[:, c[4]:c[4] + IDIM]
    ki_ref[...] = ki
    kib_ref[...] = ki.astype(BF)
    wi_ref[...] = z[:, c[5]:c[5] + NIH]


def _projections(x, mod4, wp_bf, *, tm):
    b, t, _ = x.shape
    widths = (_QW, _KW, _KW, _KW, _KW, NIH * IDIM, IDIM, IDIM, NIH)
    dtypes = (BF, F32, F32, BF, BF, BF, F32, BF, F32)
    return pl.pallas_call(
        _proj_kernel,
        grid=(b, t // tm),
        in_specs=[pl.BlockSpec((None, tm, D), lambda i, j: (i, j, 0)),
                  _mod_spec(1, 2), _mod_spec(0, 2),
                  pl.BlockSpec(wp_bf.shape, lambda i, j: (0, 0))],
        out_specs=[pl.BlockSpec((None, tm, w), lambda i, j: (i, j, 0)) for w in widths],
        out_shape=[jax.ShapeDtypeStruct((b, t, w), dt) for w, dt in zip(widths, dtypes)],
        compiler_params=_params(("arbitrary", "arbitrary")),
        name="projections",
    )(x, mod4, mod4, wp_bf)


def _attn_kernel(lim_ref, qi_ref, wi_ref, q_ref, ki_ref, k_ref, v_ref, o_ref,
                 key_scr, wb_scr, qi_scr, q_scr, cand_scr, x_scr, bias_scr, lg_scr, p_scr, m_scr, l_scr, acc_scr,
                 *, tq, tk, k_sel):
    n = pl.program_id(1)
    limit = lim_ref[n]
    nt = lax.div(limit + (tk - 1), tk)
    chunk_back = (tq - 1) // CHUNK - lax.broadcasted_iota(I32, (tq, 1), 0) // CHUNK
    lim_rows = limit - CHUNK * chunk_back

    wi = wi_ref[...]
    for h in range(NIH):
        wb_scr[h] = jnp.broadcast_to(wi[:, h:h + 1], (tq, LANES))
    for h in range(NIH):
        qi_scr[h * tq:(h + 1) * tq, :] = qi_ref[:, h * IDIM:(h + 1) * IDIM]
    for h in range(NH):
        q_scr[h * tq:(h + 1) * tq, :] = q_ref[:, h * HD:(h + 1) * HD]
    nlt = tk // LANES
    lane_iota = lax.broadcasted_iota(I32, (tq, LANES), 1)

    def score_tile(t, carry):
        off = pl.multiple_of(t * tk, tk)
        x_scr[...] = lax.dot_general(qi_scr[...], ki_ref[pl.ds(off, tk), :], _NT,
                                     preferred_element_type=F32)
        for c in range(nlt):
            cs = slice(c * LANES, (c + 1) * LANES)
            s = wb_scr[0] * jnp.maximum(x_scr[0:tq, cs], 0.0)
            for h in range(1, NIH):
                s = s + wb_scr[h] * jnp.maximum(x_scr[h * tq:(h + 1) * tq, cs], 0.0)
            bits = pltpu.bitcast(s, I32)
            key = jnp.where(bits < 0, bits ^ jnp.int32(0x7FFFFFFF), bits)
            pos = t * tk + c * LANES + lane_iota
            key_scr[t, :, cs] = jnp.where(pos < lim_rows, key, jnp.int32(INT_MIN))
        return carry

    lax.fori_loop(0, nt, score_tile, 0)

    hq = tq // 2

    def count_ge(cand):
        cand_scr[...] = jnp.broadcast_to(cand, (tq, LANES))

        def body(t, accs):
            out = []
            for half, acc in enumerate(accs):
                rs = slice(half * hq, (half + 1) * hq)
                cand_b = cand_scr[rs, :]
                for c in range(nlt):
                    acc = acc + jnp.where(key_scr[t, rs, c * LANES:(c + 1) * LANES] >= cand_b, 1.0, 0.0)
                out.append(acc)
            return tuple(out)
        zeros = jnp.zeros((hq, LANES), F32)
        accs = lax.fori_loop(0, nt, body, (zeros, zeros))
        return jnp.concatenate([jnp.sum(a, axis=1, keepdims=True) for a in accs], axis=0)

    kf = float(k_sel)
    zero = jnp.zeros((tq, 1), I32)
    res = jnp.where(count_ge(zero) >= kf, zero, jnp.int32(INT_MIN))

    def bit_step(i, res):
        cand = res | lax.shift_left(jnp.int32(1), jnp.int32(30) - i)
        return jnp.where(count_ge(cand) >= kf, cand, res)

    res = lax.fori_loop(0, 31, bit_step, res)
    thr = jnp.maximum(res, jnp.int32(INT_MIN + 1))

    rows = NH * tq
    m_scr[...] = jnp.full((rows, 1), M_INIT, F32)
    l_scr[...] = jnp.zeros((rows, LANES), F32)
    acc_scr[...] = jnp.zeros((rows, HD), F32)
    gr = QPK * tq
    c1 = (HD ** -0.5) * LOG2E
    rc = min(4 * SUBLANES, tq)

    def lane_tiles(x):
        return [x[:, c * LANES:(c + 1) * LANES] for c in range(nlt)]

    def attend_tile(t, carry):
        off = pl.multiple_of(t * tk, tk)
        bias_scr[...] = jnp.where(key_scr[t] >= thr, 0.0, NEG_BIAS)
        for g in range(NKV):
            gs = slice(g * HD, (g + 1) * HD)
            lg_scr[g] = lax.dot_general(q_scr[g * gr:(g + 1) * gr, :], k_ref[pl.ds(off, tk), gs], _NT,
                                        preferred_element_type=F32)
            for sc in range(gr // rc):
                r0 = sc * rc
                b0 = r0 % tq
                rows = slice(g * gr + r0, g * gr + r0 + rc)
                x = lg_scr[g, r0:r0 + rc, :] * c1 + bias_scr[b0:b0 + rc, :]
                m_old = m_scr[rows, :]
                m_new = jnp.maximum(m_old, jnp.max(functools.reduce(jnp.maximum, lane_tiles(x)),
                                                   axis=1, keepdims=True))
                p = jnp.exp2(x - m_new)
                a = jnp.exp2(m_old - m_new)
                l_scr[rows, :] = a * l_scr[rows, :] + functools.reduce(jnp.add, lane_tiles(p))
                m_scr[rows, :] = m_new
                acc_scr[rows, :] = a * acc_scr[rows, :]
                p_scr[g, r0:r0 + rc, :] = p.astype(BF)
            acc_scr[g * gr:(g + 1) * gr, :] += jnp.dot(p_scr[g], v_ref[pl.ds(off, tk), gs],
                                                       preferred_element_type=F32)
        return carry

    lax.fori_loop(0, nt, attend_tile, 0)
    l_tot = jnp.sum(l_scr[...], axis=1, keepdims=True)
    for h in range(NH):
        hs = slice(h * tq, (h + 1) * tq)
        o_ref[:, h * HD:(h + 1) * HD] = (acc_scr[hs, :] / l_tot[hs]).astype(BF)


def _attention(limits, qi, wi, q, ki, k, v, *, tq, tk, k_sel):
    b, t, _ = q.shape
    nblk = t // tq
    s_pad = k.shape[1]
    qblock = lambda w: pl.BlockSpec((None, tq, w), lambda i, j, lim: (i, j, 0))
    keys = lambda w: pl.BlockSpec((None, s_pad, w), lambda i, j, lim: (i, 0, 0))
    grid_spec = pltpu.PrefetchScalarGridSpec(
        num_scalar_prefetch=1,
        grid=(b, nblk),
        in_specs=[qblock(NIH * IDIM), qblock(NIH), qblock(NH * HD),
                  keys(IDIM), keys(NKV * HD), keys(NKV * HD)],
        out_specs=qblock(NH * HD),
        scratch_shapes=[pltpu.VMEM((s_pad // tk, tq, tk), I32),
                        pltpu.VMEM((NIH, tq, LANES), F32),
                        pltpu.VMEM((NIH * tq, IDIM), BF),
                        pltpu.VMEM((NH * tq, HD), BF),
                        pltpu.VMEM((tq, LANES), I32),
                        pltpu.VMEM((NIH * tq, tk), F32),
                        pltpu.VMEM((tq, tk), F32),
                        pltpu.VMEM((NKV, QPK * tq, tk), F32),
                        pltpu.VMEM((NKV, QPK * tq, tk), BF),
                        pltpu.VMEM((NH * tq, 1), F32),
                        pltpu.VMEM((NH * tq, LANES), F32),
                        pltpu.VMEM((NH * tq, HD), F32)])
    return pl.pallas_call(
        functools.partial(_attn_kernel, tq=tq, tk=tk, k_sel=k_sel),
        grid_spec=grid_spec,
        out_shape=jax.ShapeDtypeStruct((b, t, NH * HD), BF),
        compiler_params=_params(("arbitrary", "arbitrary")),
        name="attention",
    )(limits, qi, wi, q, ki, k, v)


def _post_kernel(x_ref, sc_ref, sh_ref, g1_ref, ya_ref, o_ref, wg_ref, wao_ref, wo_ref, lg_ref, lb_ref, out_ref):
    x = x_ref[...]
    h = (x * (1.0 + sc_ref[...]) + sh_ref[...]).astype(BF)
    gates = jnp.dot(h, wg_ref[...], preferred_element_type=F32)
    ga = jax.nn.sigmoid(gates[:, :D])
    gb = jax.nn.sigmoid(gates[:, D:])
    yb = jnp.dot(o_ref[...], wao_ref[...], preferred_element_type=F32)
    merged = ga * ya_ref[...] + gb * yb
    r = jnp.dot(merged.astype(BF), wo_ref[...], preferred_element_type=F32)
    out_ref[...] = _layer_norm(ALPHA * x + g1_ref[...] * r, lg_ref[...], lb_ref[...])


def _post(x, mod4, ya, o, wg_bf, wao_bf, wo_bf, ln_g, ln_b, *, tm):
    b, t, _ = x.shape
    tile = pl.BlockSpec((None, tm, D), lambda i, j: (i, j, 0))
    full = lambda shape: pl.BlockSpec(shape, lambda i, j: (0,) * len(shape))
    return pl.pallas_call(
        _post_kernel,
        grid=(b, t // tm),
        in_specs=[tile, _mod_spec(1, 2), _mod_spec(0, 2), _mod_spec(2, 2), tile, tile,
                  full((D, 2 * D)), full((NH * HD, D)), full((D, D)), full((1, D)), full((1, D))],
        out_specs=tile,
        out_shape=jax.ShapeDtypeStruct((b, t, D), F32),
        compiler_params=_params(("arbitrary", "arbitrary")),
        name="post",
    )(x, mod4, mod4, mod4, ya, o, wg_bf, wao_bf, wo_bf, ln_g.reshape(1, D), ln_b.reshape(1, D))


def _cmp_exchange(v, i, j, descending):
    hi = jnp.maximum(v[i], v[j])
    lo = jnp.minimum(v[i], v[j])
    v[i], v[j] = (hi, lo) if descending else (lo, hi)


def _bitonic_merge(v):
    n = len(v)
    j = n // 2
    while j >= 1:
        for i in range(n):
            l = i ^ j
            if l > i:
                _cmp_exchange(v, i, l, True)
        j //= 2
    return v


def _bitonic_sort(v):
    n = len(v)
    k = 2
    while k <= n:
        j = k // 2
        while j >= 1:
            for i in range(n):
                l = i ^ j
                if l > i:
                    _cmp_exchange(v, i, l, (i & k) == 0)
            j //= 2
        k *= 2
    return v


def _top16(slabs):
    a = _bitonic_sort(list(slabs))
    for shift in (4, 2, 1):
        c = [jnp.maximum(a[i], pltpu.roll(a[PTOP - 1 - i], shift, axis=0)) for i in range(PTOP)]
        a = _bitonic_merge(c)
    return a


def _bf16_pair_words(x):
    hi = pltpu.bitcast(x.astype(BF).astype(F32), I32)
    return hi | lax.shift_right_logical(hi, jnp.int32(16))


def _rows_from_words(word_row):
    packed = pltpu.bitcast(jnp.broadcast_to(word_row, (SUBLANES, LANES)), BF)
    return jnp.concatenate([packed] * (NKEYS // (2 * SUBLANES)), axis=0)


def _peer_prep_kernel(x_ref, sc_ref, sh_ref, wq_ref, gain_ref, k1_ref, k2_ref,
                      h2t_ref, r2_ref, e2_ref, nb_ref, c_ref, *, tp):
    h2 =(x_ref[...] * (1.0 + sc_ref[...]) + sh_ref[...]).reshape(tp, D)
    h2t = h2.T.astype(BF)
    h2t_ref[...] = h2t
    qt = jnp.dot(wq_ref[...], h2t, preferred_element_type=F32)
    sub = lax.broadcasted_iota(I32, (SUBLANES, tp), 0)
    ninf = jnp.float32(-jnp.inf)
    nrep = tp // LANES
    for h in range(PH):
        qh = qt[h * PQ:(h + 1) * PQ, :]
        ms = jnp.mean(qh * qh, axis=0, keepdims=True)
        gain = jnp.concatenate([gain_ref[h * PQ:(h + 1) * PQ, :]] * nrep, axis=1)
        qn = (qh * lax.rsqrt(ms + RMS_EPS) * gain).astype(BF)
        s1 = jnp.dot(k1_ref[h], qn[:PHALF], preferred_element_type=F32).reshape(PTOP, SUBLANES, tp)
        s2 = jnp.dot(k2_ref[h], qn[PHALF:], preferred_element_type=F32).reshape(PTOP, SUBLANES, tp)
        v1 = _top16([s1[i] for i in range(PTOP)])
        v2 = _top16([s2[i] for i in range(PTOP)])
        v1s = v1[SUBLANES - 1]
        for a in range(SUBLANES - 2, -1, -1):
            v1s = jnp.where(sub == a, v1[a], v1s)
        cand = []
        for b in range(PTOP):
            if b < SUBLANES:
                cand.append(jnp.where(sub < PTOP // (b + 1), v1s + v2[b], ninf))
            else:
                cand.append(jnp.where(sub == 0, v1[0] + v2[b], jnp.where(sub == 1, v1[b] + v2[0], ninf)))
        cs = _top16(cand)
        thr = cs[PTOP - 1]
        mx = cs[0]
        z = jnp.exp(cs[0] - mx)
        for k in range(1, PTOP):
            z = z + jnp.exp(cs[k] - mx)
        inv_z = 1.0 / z
        nb = jnp.zeros((PTOP, SUBLANES, tp), F32)
        rank2 = jnp.zeros((PTOP, SUBLANES, tp), F32)
        for b in range(PTOP):
            nb = nb + jnp.where(s1 + v2[b][None] >= thr[None], 1.0, 0.0)
            rank2 = rank2 + jnp.where(v2[b][None] > s2, 1.0, 0.0)
        r2_ref[h] = pltpu.bitcast(rank2.reshape(NKEYS, tp).astype(BF), I32)
        e2_ref[h] = pltpu.bitcast(jnp.exp(s2 - v2[0][None]).reshape(NKEYS, tp).astype(BF), I32)
        nb_ref[h] = _bf16_pair_words(nb.reshape(NKEYS, tp))
        c_ref[h] = _bf16_pair_words((jnp.exp(s1 - v1[0][None]) * inv_z[None]).reshape(NKEYS, tp))


def _peer_prep(x1, mod4, wqt_bf, gain_b, k1_bf, k2_bf, *, nbt, tt):
    b, t, _ = x1.shape
    tp = nbt * tt
    nb_tiles, nt_tiles = b // nbt, t // tt
    ntile = nb_tiles * nt_tiles
    tidx = lambda i, j: i * nt_tiles + j
    big = lambda r: pl.BlockSpec((None, PH, r, tp), lambda i, j: (tidx(i, j), 0, 0, 0))
    mod = lambda k: pl.BlockSpec((nbt, None, 1, D), lambda i, j: (i, k, 0, 0))
    return pl.pallas_call(
        functools.partial(_peer_prep_kernel, tp=tp),
        grid=(nb_tiles, nt_tiles),
        in_specs=[pl.BlockSpec((nbt, tt, D), lambda i, j: (i, j, 0)),
                  mod(4), mod(3),
                  pl.BlockSpec((PH * PQ, D), lambda i, j: (0, 0)),
                  pl.BlockSpec((PH * PQ, LANES), lambda i, j: (0, 0)),
                  pl.BlockSpec((PH, NKEYS, PHALF), lambda i, j: (0, 0, 0)),
                  pl.BlockSpec((PH, NKEYS, PHALF), lambda i, j: (0, 0, 0))],
        out_specs=[pl.BlockSpec((None, D, tp), lambda i, j: (tidx(i, j), 0, 0)),
                   big(NKEYS // 2), big(NKEYS // 2), big(NKEYS), big(NKEYS)],
        out_shape=[jax.ShapeDtypeStruct((ntile, D, tp), BF)]
        + [jax.ShapeDtypeStruct((ntile, PH, r, tp), I32) for r in (NKEYS // 2, NKEYS // 2, NKEYS, NKEYS)],
        compiler_params=_params(("arbitrary", "arbitrary")),
        name="peer_prep",
    )(x1, mod4, mod4, wqt_bf, gain_b, k1_bf, k2_bf)


def _gelu_tanh(a):
    k0 = -2.0 * LOG2E * 0.7978845608028654
    k1 = k0 * 0.044715
    return a / (1.0 + jnp.exp2(a * (k0 + k1 * (a * a))))


def _peer_main_kernel(h2t_ref, r2_ref, e2_ref, nb_ref, c_ref, u_ref, vt_ref, x_ref, g2_ref, lg_ref, lb_ref,
                      out_ref, acc_scr, a_scr, w_scr, *, ic, tp):
    e = pl.program_id(2)
    ne = pl.num_programs(2)

    @pl.when(e == 0)
    def _():
        acc_scr[...] = jnp.zeros_like(acc_scr)

    h2t = h2t_ref[...]

    def expert_inputs(il):
        r0 = pl.multiple_of(jnp.minimum(il, ic - 1) * NKEYS, NKEYS)
        return jnp.dot(u_ref[pl.ds(r0, NKEYS), :], h2t, preferred_element_type=F32)

    def gated(il, a_ref):
        i = e * ic + il
        r0 = pl.multiple_of(il * NKEYS, NKEYS)
        nb_rows = [nb_ref[h, pl.ds(i, 1), :] for h in range(PH)]
        c_rows = [c_ref[h, pl.ds(i, 1), :] for h in range(PH)]
        for c in range(tp // LANES):
            cs = slice(c * LANES, (c + 1) * LANES)
            gate = None
            for h in range(PH):
                sel = jnp.where(pltpu.bitcast(r2_ref[h, :, cs], BF) < _rows_from_words(nb_rows[h][:, cs]),
                                pltpu.bitcast(e2_ref[h, :, cs], BF), jnp.zeros((), BF))
                term = sel * _rows_from_words(c_rows[h][:, cs])
                gate = term if gate is None else gate + term
            act = _gelu_tanh(a_ref[:, cs]).astype(BF)
            w_scr[pl.ds(r0, NKEYS), cs] = act * gate

    a_scr[0] = expert_inputs(0)

    def key_pair(ip, carry):
        a_scr[1] = expert_inputs(2 * ip + 1)
        gated(2 * ip, a_scr.at[0])
        a_scr[0] = expert_inputs(2 * ip + 2)
        gated(2 * ip + 1, a_scr.at[1])
        return carry

    lax.fori_loop(0, ic // 2, key_pair, 0)
    acc_scr[...] += jnp.dot(vt_ref[...], w_scr[...], preferred_element_type=F32)

    @pl.when(e == ne - 1)
    def _():
        ff = acc_scr[...].T.reshape(out_ref.shape)
        out_ref[...] = _layer_norm(ALPHA * x_ref[...] + g2_ref[...] * ff, lg_ref[...], lb_ref[...])


def _peer_main(h2t, s2, e2, tc, c, u_bf, vt_bf, x1, mod4, ln_g, ln_b, *, nbt, tt, ic):
    b, t, _ = x1.shape
    tp = nbt * tt
    nb_tiles, nt_tiles = b // nbt, t // tt
    tidx = lambda i, j: i * nt_tiles + j
    big = lambda r: pl.BlockSpec((None, PH, r, tp), lambda i, j, e: (tidx(i, j), 0, 0, 0))
    xt = pl.BlockSpec((nbt, tt, D), lambda i, j, e: (i, j, 0))
    return pl.pallas_call(
        functools.partial(_peer_main_kernel, ic=ic, tp=tp),
        grid=(nb_tiles, nt_tiles, NKEYS // ic),
        in_specs=[pl.BlockSpec((None, D, tp), lambda i, j, e: (tidx(i, j), 0, 0)),
                  big(NKEYS // 2), big(NKEYS // 2), big(NKEYS), big(NKEYS),
                  pl.BlockSpec((ic * NKEYS, D), lambda i, j, e: (e, 0)),
                  pl.BlockSpec((D, ic * NKEYS), lambda i, j, e: (0, e)),
                  xt,
                  pl.BlockSpec((nbt, None, 1, D), lambda i, j, e: (i, 5, 0, 0)),
                  pl.BlockSpec((1, 1, D), lambda i, j, e: (0, 0, 0)),
                  pl.BlockSpec((1, 1, D), lambda i, j, e: (0, 0, 0))],
        out_specs=xt,
        out_shape=jax.ShapeDtypeStruct((b, t, D), F32),
        scratch_shapes=[pltpu.VMEM((D, tp), F32), pltpu.VMEM((2, NKEYS, tp), F32),
                        pltpu.VMEM((ic * NKEYS, tp), BF)],
        compiler_params=_params(("arbitrary", "arbitrary", "arbitrary")),
        name="peer_main",
    )(h2t, s2, e2, tc, c, u_bf, vt_bf, x1, mod4, ln_g.reshape(1, 1, D), ln_b.reshape(1, 1, D))


def _layer(x, mod, hist, past, w, *, tm, tq, tk, peer_nbt, peer_tt, peer_ic):
    b, t, _ = x.shape
    mod4 = mod.reshape(b, 6, 1, D)
    ya, conv_state = _conv_mixer(x, mod4, hist, w["w3"], w["conv_w"], w["wco"], tm=tm)
    q, k, v, kb, vb, qi, ki, kib, wi = _projections(x, mod4, w["wp"], tm=tm)

    if past is None:
        k_all, v_all, ki_all = kb, vb, kib
        s_len = t
        limits = ((jnp.arange(t // tq, dtype=I32) * tq + (tq - 1)) // CHUNK + 1) * CHUNK
        k_sel = min(TOPK_MAX, t // 4)
    else:
        pk, pv, pki = past
        k_all = jnp.concatenate([pk.reshape(b, -1, NKV * HD).astype(BF), kb], axis=1)
        v_all = jnp.concatenate([pv.reshape(b, -1, NKV * HD).astype(BF), vb], axis=1)
        ki_all = jnp.concatenate([pki.astype(BF), kib], axis=1)
        s_len = k_all.shape[1]
        limits = jnp.full((t // tq,), s_len, I32)
        k_sel = min(TOPK_MAX, s_len // 4)
    s_pad = -(-s_len // tk) * tk
    pad = ((0, 0), (0, s_pad - s_len), (0, 0))
    if s_pad != s_len:
        k_all, v_all, ki_all = (jnp.pad(a, pad) for a in (k_all, v_all, ki_all))
    o = _attention(limits, qi, wi, q, ki_all, k_all, v_all, tq=tq, tk=tk, k_sel=k_sel)

    x1 = _post(x, mod4, ya, o, w["wg"], w["wao"], w["wo"], w["ln1_g"], w["ln1_b"], tm=tm)

    h2t, s2, e2, tc, c = _peer_prep(x1, mod4, w["wqt"], w["gain_b"], w["k1"], w["k2"], nbt=peer_nbt, tt=peer_tt)
    y = _peer_main(h2t, s2, e2, tc, c, w["u"], w["vt"], x1, mod4, w["ln2_g"], w["ln2_b"],
                   nbt=peer_nbt, tt=peer_tt, ic=peer_ic)
    return y, k.reshape(b, t, NKV, HD), v.reshape(b, t, NKV, HD), ki, conv_state


def _prep_weights(w_mix_in, conv_w, w_conv_out, w_attn_out, w_o, ln1_g, ln1_b, w_peer_q, peer_q_gain,
                  sub_keys_1, sub_keys_2, expert_u, expert_v, ln2_g, ln2_b):
    c0 = 3 * DC
    wq = w_mix_in[:, c0:c0 + 2 * _QW + 2 * _KW]
    c1 = c0 + 2 * _QW + 2 * _KW
    wki = jnp.pad(w_mix_in[:, c1:c1 + IDIM], ((0, 0), (0, LANES - IDIM)))
    wwi = jnp.pad(w_mix_in[:, c1 + IDIM:c1 + IDIM + NIH], ((0, 0), (0, LANES - NIH)))
    c2 = c1 + IDIM + NIH
    return dict(
        w3=w_mix_in[:, :c0].astype(BF),
        wp=jnp.concatenate([wq, wki, wwi], axis=1).astype(BF),
        wg=w_mix_in[:, c2:c2 + 2 * D].astype(BF),
        conv_w=conv_w, wco=w_conv_out.astype(BF), wao=w_attn_out.astype(BF), wo=w_o.astype(BF),
        ln1_g=ln1_g, ln1_b=ln1_b,
        wqt=w_peer_q.T.astype(BF),
        gain_b=jnp.broadcast_to(peer_q_gain.reshape(PH * PQ, 1), (PH * PQ, LANES)),
        k1=sub_keys_1.astype(BF), k2=sub_keys_2.astype(BF),
        u=expert_u.astype(BF), vt=expert_v.T.astype(BF),
        ln2_g=ln2_g, ln2_b=ln2_b)


def kernel(x_prompt, x_sample, c_prompt, c_sample, cache_k, cache_v, cache_idx_k, state_conv, w_ada, b_ada,
           w_mix_in, conv_w, w_conv_out, w_attn_out, w_o, ln1_g, ln1_b, w_peer_q, peer_q_gain, sub_keys_1,
           sub_keys_2, expert_u, expert_v, ln2_g, ln2_b):
    bp, bs = x_prompt.shape[0], x_sample.shape[0]
    ts = x_sample.shape[1]
    mod = _modulation(jnp.concatenate([c_prompt, c_sample], axis=0), w_ada[0].astype(BF), b_ada[0])
    w = _prep_weights(w_mix_in[0], conv_w[0], w_conv_out[0], w_attn_out[0], w_o[0], ln1_g[0], ln1_b[0],
                      w_peer_q[0], peer_q_gain[0], sub_keys_1[0], sub_keys_2[0], expert_u[0], expert_v[0],
                      ln2_g[0], ln2_b[0])
    zero_hist = jnp.zeros((bp, 2, DC), F32)
    tmp = min(512, x_prompt.shape[1])
    yp, kp, vp, kip, cp = _layer(x_prompt, mod[:bp], zero_hist, None, w,
                                 tm=tmp, tq=2 * CHUNK, tk=512, peer_nbt=1, peer_tt=tmp, peer_ic=8)
    ys, ks, vs, kis, cs = _layer(x_sample, mod[bp:], state_conv[0], (cache_k[0], cache_v[0], cache_idx_k[0]), w,
                                 tm=ts, tq=ts, tk=512, peer_nbt=bs, peer_tt=ts, peer_ic=8)
    return (yp, ys, kp[None], vp[None], kip[None], cp[None], ks[None], vs[None], kis[None], cs[None])
```

```python
import functools

import jax
import jax.numpy as jnp
from jax import lax
from jax.experimental import pallas as pl
from jax.experimental.pallas import tpu as pltpu

BF = jnp.bfloat16
F32 = jnp.float32
I32 = jnp.int32

D = 1024
DC = 1024
CHUNK = 64
NH = 8
NKV = 2
QPK = NH // NKV
HD = 128
NIH = 16
IDIM = 64
TOPK_MAX = 256
PH = 8
PQ = 256
PHALF = PQ // 2
NKEYS = 128
NEXP = NKEYS * NKEYS
PTOP = 16
LN_EPS = 1e-5
RMS_EPS = 1e-6
ALPHA = 2.0 ** 0.25

LANES = 128
SUBLANES = 8
VMEM_LIMIT = 56 * 1024 * 1024
INT_MIN = -2 ** 31
NEG_BIAS = -3e30
M_INIT = -1e30
LOG2E = 1.4426950408889634

_NT = (((1,), (1,)), ((), ()))


def _params(sem):
    return pltpu.CompilerParams(dimension_semantics=sem, vmem_limit_bytes=VMEM_LIMIT)


def _layer_norm(y, g, b):
    mu = jnp.mean(y, axis=-1, keepdims=True)
    yc = y - mu
    var = jnp.mean(yc * yc, axis=-1, keepdims=True)
    return yc * lax.rsqrt(var + LN_EPS) * g + b


def _mod_kernel(c_ref, w_ref, b_ref, o_ref):
    c = c_ref[...]
    s = c * jax.nn.sigmoid(c)
    o_ref[...] = jnp.dot(s.astype(BF), w_ref[...], preferred_element_type=F32) + b_ref[...]


def _modulation(c, w_ada_bf, b_ada):
    nb = c.shape[0]
    n_out = w_ada_bf.shape[1]
    tn = D
    return pl.pallas_call(
        _mod_kernel,
        grid=(n_out // tn,),
        in_specs=[pl.BlockSpec((nb, D), lambda j: (0, 0)),
                  pl.BlockSpec((D, tn), lambda j: (0, j)),
                  pl.BlockSpec((1, tn), lambda j: (0, j))],
        out_specs=pl.BlockSpec((nb, tn), lambda j: (0, j)),
        out_shape=jax.ShapeDtypeStruct((nb, n_out), F32),
        compiler_params=_params(("arbitrary",)),
        name="modulation",
    )(c, w_ada_bf, b_ada.reshape(1, n_out))


def _mod_spec(k, grid_rank):
    if grid_rank == 2:
        return pl.BlockSpec((None, None, 1, D), lambda b, t: (b, k, 0, 0))
    raise ValueError(grid_rank)


def _conv_kernel(x_ref, sc_ref, sh_ref, hist_ref, w3_ref, cw_ref, wco_ref, ya_ref, cs_ref, ubuf, *, tm):
    t = pl.program_id(1)
    h = (x_ref[...] * (1.0 + sc_ref[...]) + sh_ref[...]).astype(BF)
    z = jnp.dot(h, w3_ref[...], preferred_element_type=F32)
    xin = z[:, :DC]
    gb = z[:, DC:2 * DC]
    gc = z[:, 2 * DC:]

    @pl.when(t == 0)
    def _():
        ubuf[0:SUBLANES, :] = jnp.zeros((SUBLANES, DC), F32)
        ubuf[SUBLANES - 2:SUBLANES, :] = hist_ref[...]

    ubuf[SUBLANES:SUBLANES + tm, :] = gc * xin
    cw = cw_ref[...]
    y = (cw[0:1] * ubuf[SUBLANES - 2:SUBLANES - 2 + tm, :]
         + cw[1:2] * ubuf[SUBLANES - 1:SUBLANES - 1 + tm, :]
         + cw[2:3] * ubuf[SUBLANES:SUBLANES + tm, :])
    ya_ref[...] = jnp.dot((gb * y).astype(BF), wco_ref[...], preferred_element_type=F32)
    tail = ubuf[tm:tm + SUBLANES, :]
    ubuf[0:SUBLANES, :] = tail
    cs_ref[...] = tail[SUBLANES - 2:SUBLANES]


def _conv_mixer(x, mod4, hist, w3_bf, conv_w, wco_bf, *, tm):
    b, t, _ = x.shape
    return pl.pallas_call(
        functools.partial(_conv_kernel, tm=tm),
        grid=(b, t // tm),
        in_specs=[pl.BlockSpec((None, tm, D), lambda i, j: (i, j, 0)),
                  _mod_spec(1, 2), _mod_spec(0, 2),
                  pl.BlockSpec((None, 2, DC), lambda i, j: (i, 0, 0)),
                  pl.BlockSpec((D, 3 * DC), lambda i, j: (0, 0)),
                  pl.BlockSpec((3, DC), lambda i, j: (0, 0)),
                  pl.BlockSpec((DC, D), lambda i, j: (0, 0))],
        out_specs=[pl.BlockSpec((None, tm, D), lambda i, j: (i, j, 0)),
                   pl.BlockSpec((None, 2, DC), lambda i, j: (i, 0, 0))],
        out_shape=[jax.ShapeDtypeStruct((b, t, D), F32),
                   jax.ShapeDtypeStruct((b, 2, DC), F32)],
        scratch_shapes=[pltpu.VMEM((tm + SUBLANES, DC), F32)],
        compiler_params=_params(("arbitrary", "arbitrary")),
        name="conv_mixer",
    )(x, mod4, mod4, hist, w3_bf, conv_w, wco_bf)


_QW = NH * HD
_KW = NKV * HD
_PROJ_COLS = (0, _QW, _QW + _KW, _QW + 2 * _KW, 2 * _QW + 2 * _KW,
              2 * _QW + 2 * _KW + LANES, 2 * _QW + 2 * _KW + 2 * LANES)


def _proj_kernel(x_ref, sc_ref, sh_ref, w_ref, q_ref, k_ref, v_ref, kb_ref, vb_ref, qi_ref, ki_ref, kib_ref, wi_ref):
    h = (x_ref[...] * (1.0 + sc_ref[...]) + sh_ref[...]).astype(BF)
    z = jnp.dot(h, w_ref[...], preferred_element_type=F32)
    c = _PROJ_COLS
    q_ref[...] = z[:, c[0]:c[1]].astype(BF)
    k = z[:, c[1]:c[2]]
    v = z[:, c[2]:c[3]]
    k_ref[...] = k
    v_ref[...] = v
    kb_ref[...] = k.astype(BF)
    vb_ref[...] = v.astype(BF)
    qi_ref[...] = z[:, c[3]:c[4]].astype(BF)
    ki = z[:, c[4]:c[4] + IDIM]
    ki_ref[...] = ki
    kib_ref[...] = ki.astype(BF)
    wi_ref[...] = z[:, c[5]:c[5] + NIH]


def _projections(x, mod4, wp_bf, *, tm):
    b, t, _ = x.shape
    widths = (_QW, _KW, _KW, _KW, _KW, NIH * IDIM, IDIM, IDIM, NIH)
    dtypes = (BF, F32, F32, BF, BF, BF, F32, BF, F32)
    return pl.pallas_call(
        _proj_kernel,
        grid=(b, t // tm),
        in_specs=[pl.BlockSpec((None, tm, D), lambda i, j: (i, j, 0)),
                  _mod_spec(1, 2), _mod_spec(0, 2),
                  pl.BlockSpec(wp_bf.shape, lambda i, j: (0, 0))],
        out_specs=[pl.BlockSpec((None, tm, w), lambda i, j: (i, j, 0)) for w in widths],
        out_shape=[jax.ShapeDtypeStruct((b, t, w), dt) for w, dt in zip(widths, dtypes)],
        compiler_params=_params(("arbitrary", "arbitrary")),
        name="projections",
    )(x, mod4, mod4, wp_bf)


def _attn_kernel(lim_ref, qi_ref, wi_ref, q_ref, ki_ref, k_ref, v_ref, o_ref,
                 key_scr, wb_scr, qi_scr, q_scr, cand_scr, x_scr, bias_scr, lg_scr, p_scr, m_scr, l_scr, acc_scr,
                 *, tq, tk, k_sel):
    n = pl.program_id(1)
    limit = lim_ref[n]
    nt = lax.div(limit + (tk - 1), tk)
    chunk_back = (tq - 1) // CHUNK - lax.broadcasted_iota(I32, (tq, 1), 0) // CHUNK
    lim_rows = limit - CHUNK * chunk_back

    wi = wi_ref[...]
    for h in range(NIH):
        wb_scr[h] = jnp.broadcast_to(wi[:, h:h + 1], (tq, LANES))
    for h in range(NIH):
        qi_scr[h * tq:(h + 1) * tq, :] = qi_ref[:, h * IDIM:(h + 1) * IDIM]
    for h in range(NH):
        q_scr[h * tq:(h + 1) * tq, :] = q_ref[:, h * HD:(h + 1) * HD]
    nlt = tk // LANES
    lane_iota = lax.broadcasted_iota(I32, (tq, LANES), 1)

    def score_tile(t, carry):
        off = pl.multiple_of(t * tk, tk)
        x_scr[...] = lax.dot_general(qi_scr[...], ki_ref[pl.ds(off, tk), :], _NT,
                                     preferred_element_type=F32)
        for c in range(nlt):
            cs = slice(c * LANES, (c + 1) * LANES)
            s = wb_scr[0] * jnp.maximum(x_scr[0:tq, cs], 0.0)
            for h in range(1, NIH):
                s = s + wb_scr[h] * jnp.maximum(x_scr[h * tq:(h + 1) * tq, cs], 0.0)
            bits = pltpu.bitcast(s, I32)
            key = jnp.where(bits < 0, bits ^ jnp.int32(0x7FFFFFFF), bits)
            pos = t * tk + c * LANES + lane_iota
            key_scr[t, :, cs] = jnp.where(pos < lim_rows, key, jnp.int32(INT_MIN))
        return carry

    lax.fori_loop(0, nt, score_tile, 0)

    hq = tq // 2

    def count_ge(cand):
        cand_scr[...] = jnp.broadcast_to(cand, (tq, LANES))

        def body(t, accs):
            out = []
            for half, acc in enumerate(accs):
                rs = slice(half * hq, (half + 1) * hq)
                cand_b = cand_scr[rs, :]
                for c in range(nlt):
                    acc = acc + jnp.where(key_scr[t, rs, c * LANES:(c + 1) * LANES] >= cand_b, 1.0, 0.0)
                out.append(acc)
            return tuple(out)
        zeros = jnp.zeros((hq, LANES), F32)
        accs = lax.fori_loop(0, nt, body, (zeros, zeros))
        return jnp.concatenate([jnp.sum(a, axis=1, keepdims=True) for a in accs], axis=0)

    kf = float(k_sel)
    zero = jnp.zeros((tq, 1), I32)
    res = jnp.where(count_ge(zero) >= kf, zero, jnp.int32(INT_MIN))

    def bit_step(i, res):
        cand = res | lax.shift_left(jnp.int32(1), jnp.int32(30) - i)
        return jnp.where(count_ge(cand) >= kf, cand, res)

    res = lax.fori_loop(0, 31, bit_step, res)
    thr = jnp.maximum(res, jnp.int32(INT_MIN + 1))

    rows = NH * tq
    m_scr[...] = jnp.full((rows, 1), M_INIT, F32)
    l_scr[...] = jnp.zeros((rows, LANES), F32)
    acc_scr[...] = jnp.zeros((rows, HD), F32)
    gr = QPK * tq
    c1 = (HD ** -0.5) * LOG2E
    rc = min(4 * SUBLANES, tq)

    def lane_tiles(x):
        return [x[:, c * LANES:(c + 1) * LANES] for c in range(nlt)]

    def attend_tile(t, carry):
        off = pl.multiple_of(t * tk, tk)
        bias_scr[...] = jnp.where(key_scr[t] >= thr, 0.0, NEG_BIAS)
        for g in range(NKV):
            gs = slice(g * HD, (g + 1) * HD)
            lg_scr[g] = lax.dot_general(q_scr[g * gr:(g + 1) * gr, :], k_ref[pl.ds(off, tk), gs], _NT,
                                        preferred_element_type=F32)
            for sc in range(gr // rc):
                r0 = sc * rc
                b0 = r0 % tq
                rows = slice(g * gr + r0, g * gr + r0 + rc)
                x = lg_scr[g, r0:r0 + rc, :] * c1 + bias_scr[b0:b0 + rc, :]
                m_old = m_scr[rows, :]
                m_new = jnp.maximum(m_old, jnp.max(functools.reduce(jnp.maximum, lane_tiles(x)),
                                                   axis=1, keepdims=True))
                p = jnp.exp2(x - m_new)
                a = jnp.exp2(m_old - m_new)
                l_scr[rows, :] = a * l_scr[rows, :] + functools.reduce(jnp.add, lane_tiles(p))
                m_scr[rows, :] = m_new
                acc_scr[rows, :] = a * acc_scr[rows, :]
                p_scr[g, r0:r0 + rc, :] = p.astype(BF)
            acc_scr[g * gr:(g + 1) * gr, :] += jnp.dot(p_scr[g], v_ref[pl.ds(off, tk), gs],
                                                       preferred_element_type=F32)
        return carry

    lax.fori_loop(0, nt, attend_tile, 0)
    l_tot = jnp.sum(l_scr[...], axis=1, keepdims=True)
    for h in range(NH):
        hs = slice(h * tq, (h + 1) * tq)
        o_ref[:, h * HD:(h + 1) * HD] = (acc_scr[hs, :] / l_tot[hs]).astype(BF)


def _attention(limits, qi, wi, q, ki, k, v, *, tq, tk, k_sel):
    b, t, _ = q.shape
    nblk = t // tq
    s_pad = k.shape[1]
    qblock = lambda w: pl.BlockSpec((None, tq, w), lambda i, j, lim: (i, j, 0))
    keys = lambda w: pl.BlockSpec((None, s_pad, w), lambda i, j, lim: (i, 0, 0), pipeline_mode=pl.Buffered(1))
    grid_spec = pltpu.PrefetchScalarGridSpec(
        num_scalar_prefetch=1,
        grid=(b, nblk),
        in_specs=[qblock(NIH * IDIM), qblock(NIH), qblock(NH * HD),
                  keys(IDIM), keys(NKV * HD), keys(NKV * HD)],
        out_specs=qblock(NH * HD),
        scratch_shapes=[pltpu.VMEM((s_pad // tk, tq, tk), I32),
                        pltpu.VMEM((NIH, tq, LANES), F32),
                        pltpu.VMEM((NIH * tq, IDIM), BF),
                        pltpu.VMEM((NH * tq, HD), BF),
                        pltpu.VMEM((tq, LANES), I32),
                        pltpu.VMEM((NIH * tq, tk), F32),
                        pltpu.VMEM((tq, tk), F32),
                        pltpu.VMEM((NKV, QPK * tq, tk), F32),
                        pltpu.VMEM((NKV, QPK * tq, tk), BF),
                        pltpu.VMEM((NH * tq, 1), F32),
                        pltpu.VMEM((NH * tq, LANES), F32),
                        pltpu.VMEM((NH * tq, HD), F32)])
    return pl.pallas_call(
        functools.partial(_attn_kernel, tq=tq, tk=tk, k_sel=k_sel),
        grid_spec=grid_spec,
        out_shape=jax.ShapeDtypeStruct((b, t, NH * HD), BF),
        compiler_params=_params(("arbitrary", "arbitrary")),
        name="attention",
    )(limits, qi, wi, q, ki, k, v)


def _post_kernel(x_ref, sc_ref, sh_ref, g1_ref, ya_ref, o_ref, wg_ref, wao_ref, wo_ref, lg_ref, lb_ref, out_ref):
    x = x_ref[...]
    h = (x * (1.0 + sc_ref[...]) + sh_ref[...]).astype(BF)
    gates = jnp.dot(h, wg_ref[...], preferred_element_type=F32)
    ga = jax.nn.sigmoid(gates[:, :D])
    gb = jax.nn.sigmoid(gates[:, D:])
    yb = jnp.dot(o_ref[...], wao_ref[...], preferred_element_type=F32)
    merged = ga * ya_ref[...] + gb * yb
    r = jnp.dot(merged.astype(BF), wo_ref[...], preferred_element_type=F32)
    out_ref[...] = _layer_norm(ALPHA * x + g1_ref[...] * r, lg_ref[...], lb_ref[...])


def _post(x, mod4, ya, o, wg_bf, wao_bf, wo_bf, ln_g, ln_b, *, tm):
    b, t, _ = x.shape
    tile = pl.BlockSpec((None, tm, D), lambda i, j: (i, j, 0))
    full = lambda shape: pl.BlockSpec(shape, lambda i, j: (0,) * len(shape))
    return pl.pallas_call(
        _post_kernel,
        grid=(b, t // tm),
        in_specs=[tile, _mod_spec(1, 2), _mod_spec(0, 2), _mod_spec(2, 2), tile, tile,
                  full((D, 2 * D)), full((NH * HD, D)), full((D, D)), full((1, D)), full((1, D))],
        out_specs=tile,
        out_shape=jax.ShapeDtypeStruct((b, t, D), F32),
        compiler_params=_params(("arbitrary", "arbitrary")),
        name="post",
    )(x, mod4, mod4, mod4, ya, o, wg_bf, wao_bf, wo_bf, ln_g.reshape(1, D), ln_b.reshape(1, D))


def _cmp_exchange(v, i, j, descending):
    hi = jnp.maximum(v[i], v[j])
    lo = jnp.minimum(v[i], v[j])
    v[i], v[j] = (hi, lo) if descending else (lo, hi)


def _bitonic_merge(v):
    n = len(v)
    j = n // 2
    while j >= 1:
        for i in range(n):
            l = i ^ j
            if l > i:
                _cmp_exchange(v, i, l, True)
        j //= 2
    return v


def _bitonic_sort(v):
    n = len(v)
    k = 2
    while k <= n:
        j = k // 2
        while j >= 1:
            for i in range(n):
                l = i ^ j
                if l > i:
                    _cmp_exchange(v, i, l, (i & k) == 0)
            j //= 2
        k *= 2
    return v


def _top16(slabs):
    a = _bitonic_sort(list(slabs))
    for shift in (4, 2, 1):
        c = [jnp.maximum(a[i], pltpu.roll(a[PTOP - 1 - i], shift, axis=0)) for i in range(PTOP)]
        a = _bitonic_merge(c)
    return a


def _bf16_pair_words(x):
    hi = pltpu.bitcast(x.astype(BF).astype(F32), I32)
    return hi | lax.shift_right_logical(hi, jnp.int32(16))


def _rows_from_words(word_row):
    packed = pltpu.bitcast(jnp.broadcast_to(word_row, (SUBLANES, LANES)), BF)
    return jnp.concatenate([packed] * (NKEYS // (2 * SUBLANES)), axis=0)


def _peer_prep_kernel(x_ref, sc_ref, sh_ref, wq_ref, gain_ref, k1_ref, k2_ref,
                      h2t_ref, r2_ref, e2_ref, nb_ref, c_ref, *, tp):
    h2 =(x_ref[...] * (1.0 + sc_ref[...]) + sh_ref[...]).reshape(tp, D)
    h2t = h2.T.astype(BF)
    h2t_ref[...] = h2t
    qt = jnp.dot(wq_ref[...], h2t, preferred_element_type=F32)
    sub = lax.broadcasted_iota(I32, (SUBLANES, tp), 0)
    ninf = jnp.float32(-jnp.inf)
    nrep = tp // LANES
    for h in range(PH):
        qh = qt[h * PQ:(h + 1) * PQ, :]
        ms = jnp.mean(qh * qh, axis=0, keepdims=True)
        gain = jnp.concatenate([gain_ref[h * PQ:(h + 1) * PQ, :]] * nrep, axis=1)
        qn = (qh * lax.rsqrt(ms + RMS_EPS) * gain).astype(BF)
        s1 = jnp.dot(k1_ref[h], qn[:PHALF], preferred_element_type=F32).reshape(PTOP, SUBLANES, tp)
        s2 = jnp.dot(k2_ref[h], qn[PHALF:], preferred_element_type=F32).reshape(PTOP, SUBLANES, tp)
        v1 = _top16([s1[i] for i in range(PTOP)])
        v2 = _top16([s2[i] for i in range(PTOP)])
        v1s = v1[SUBLANES - 1]
        for a in range(SUBLANES - 2, -1, -1):
            v1s = jnp.where(sub == a, v1[a], v1s)
        cand = []
        for b in range(PTOP):
            if b < SUBLANES:
                cand.append(jnp.where(sub < PTOP // (b + 1), v1s + v2[b], ninf))
            else:
                cand.append(jnp.where(sub == 0, v1[0] + v2[b], jnp.where(sub == 1, v1[b] + v2[0], ninf)))
        cs = _top16(cand)
        thr = cs[PTOP - 1]
        mx = cs[0]
        z = jnp.exp(cs[0] - mx)
        for k in range(1, PTOP):
            z = z + jnp.exp(cs[k] - mx)
        inv_z = 1.0 / z
        nb = jnp.zeros((PTOP, SUBLANES, tp), F32)
        rank2 = jnp.zeros((PTOP, SUBLANES, tp), F32)
        for b in range(PTOP):
            nb = nb + jnp.where(s1 + v2[b][None] >= thr[None], 1.0, 0.0)
            rank2 = rank2 + jnp.where(v2[b][None] > s2, 1.0, 0.0)
        r2_ref[h] = pltpu.bitcast(rank2.reshape(NKEYS, tp).astype(BF), I32)
        e2_ref[h] = pltpu.bitcast(jnp.exp(s2 - v2[0][None]).reshape(NKEYS, tp).astype(BF), I32)
        nb_ref[h] = _bf16_pair_words(nb.reshape(NKEYS, tp))
        c_ref[h] = _bf16_pair_words((jnp.exp(s1 - v1[0][None]) * inv_z[None]).reshape(NKEYS, tp))


def _peer_prep(x1, mod4, wqt_bf, gain_b, k1_bf, k2_bf, *, nbt, tt):
    b, t, _ = x1.shape
    tp = nbt * tt
    nb_tiles, nt_tiles = b // nbt, t // tt
    ntile = nb_tiles * nt_tiles
    tidx = lambda i, j: i * nt_tiles + j
    big = lambda r: pl.BlockSpec((None, PH, r, tp), lambda i, j: (tidx(i, j), 0, 0, 0))
    mod = lambda k: pl.BlockSpec((nbt, None, 1, D), lambda i, j: (i, k, 0, 0))
    return pl.pallas_call(
        functools.partial(_peer_prep_kernel, tp=tp),
        grid=(nb_tiles, nt_tiles),
        in_specs=[pl.BlockSpec((nbt, tt, D), lambda i, j: (i, j, 0)),
                  mod(4), mod(3),
                  pl.BlockSpec((PH * PQ, D), lambda i, j: (0, 0)),
                  pl.BlockSpec((PH * PQ, LANES), lambda i, j: (0, 0)),
                  pl.BlockSpec((PH, NKEYS, PHALF), lambda i, j: (0, 0, 0)),
                  pl.BlockSpec((PH, NKEYS, PHALF), lambda i, j: (0, 0, 0))],
        out_specs=[pl.BlockSpec((None, D, tp), lambda i, j: (tidx(i, j), 0, 0)),
                   big(NKEYS // 2), big(NKEYS // 2), big(NKEYS), big(NKEYS)],
        out_shape=[jax.ShapeDtypeStruct((ntile, D, tp), BF)]
        + [jax.ShapeDtypeStruct((ntile, PH, r, tp), I32) for r in (NKEYS // 2, NKEYS // 2, NKEYS, NKEYS)],
        compiler_params=_params(("arbitrary", "arbitrary")),
        name="peer_prep",
    )(x1, mod4, mod4, wqt_bf, gain_b, k1_bf, k2_bf)


def _gelu_tanh(a):
    k0 = -2.0 * LOG2E * 0.7978845608028654
    k1 = k0 * 0.044715
    return a / (1.0 + jnp.exp2(a * (k0 + k1 * (a * a))))


def _peer_main_kernel(h2t_ref, r2_ref, e2_ref, nb_ref, c_ref, u_ref, vt_ref, x_ref, g2_ref, lg_ref, lb_ref,
                      out_ref, acc_scr, a_scr, w_scr, *, ic, tp):
    e = pl.program_id(2)
    ne = pl.num_programs(2)

    @pl.when(e == 0)
    def _():
        acc_scr[...] = jnp.zeros_like(acc_scr)

    h2t = h2t_ref[...]

    def expert_inputs(il):
        r0 = pl.multiple_of(jnp.minimum(il, ic - 1) * NKEYS, NKEYS)
        return jnp.dot(u_ref[pl.ds(r0, NKEYS), :], h2t, preferred_element_type=F32)

    def gated(il, a_ref):
        i = e * ic + il
        r0 = pl.multiple_of(il * NKEYS, NKEYS)
        nb_rows = [nb_ref[h, pl.ds(i, 1), :] for h in range(PH)]
        c_rows = [c_ref[h, pl.ds(i, 1), :] for h in range(PH)]
        for c in range(tp // LANES):
            cs = slice(c * LANES, (c + 1) * LANES)
            gate = None
            for h in range(PH):
                sel = jnp.where(pltpu.bitcast(r2_ref[h, :, cs], BF) < _rows_from_words(nb_rows[h][:, cs]),
                                pltpu.bitcast(e2_ref[h, :, cs], BF), jnp.zeros((), BF))
                term = sel * _rows_from_words(c_rows[h][:, cs])
                gate = term if gate is None else gate + term
            act = _gelu_tanh(a_ref[:, cs]).astype(BF)
            w_scr[pl.ds(r0, NKEYS), cs] = act * gate

    a_scr[0] = expert_inputs(0)

    def key_pair(ip, carry):
        a_scr[1] = expert_inputs(2 * ip + 1)
        gated(2 * ip, a_scr.at[0])
        a_scr[0] = expert_inputs(2 * ip + 2)
        gated(2 * ip + 1, a_scr.at[1])
        return carry

    lax.fori_loop(0, ic // 2, key_pair, 0)
    acc_scr[...] += jnp.dot(vt_ref[...], w_scr[...], preferred_element_type=F32)

    @pl.when(e == ne - 1)
    def _():
        ff = acc_scr[...].T.reshape(out_ref.shape)
        out_ref[...] = _layer_norm(ALPHA * x_ref[...] + g2_ref[...] * ff, lg_ref[...], lb_ref[...])


def _peer_main(h2t, s2, e2, tc, c, u_bf, vt_bf, x1, mod4, ln_g, ln_b, *, nbt, tt, ic):
    b, t, _ = x1.shape
    tp = nbt * tt
    nb_tiles, nt_tiles = b // nbt, t // tt
    tidx = lambda i, j: i * nt_tiles + j
    big = lambda r: pl.BlockSpec((None, PH, r, tp), lambda i, j, e: (tidx(i, j), 0, 0, 0))
    xt = pl.BlockSpec((nbt, tt, D), lambda i, j, e: (i, j, 0))
    return pl.pallas_call(
        functools.partial(_peer_main_kernel, ic=ic, tp=tp),
        grid=(nb_tiles, nt_tiles, NKEYS // ic),
        in_specs=[pl.BlockSpec((None, D, tp), lambda i, j, e: (tidx(i, j), 0, 0)),
                  big(NKEYS // 2), big(NKEYS // 2), big(NKEYS), big(NKEYS),
                  pl.BlockSpec((ic * NKEYS, D), lambda i, j, e: (e, 0)),
                  pl.BlockSpec((D, ic * NKEYS), lambda i, j, e: (0, e)),
                  xt,
                  pl.BlockSpec((nbt, None, 1, D), lambda i, j, e: (i, 5, 0, 0)),
                  pl.BlockSpec((1, 1, D), lambda i, j, e: (0, 0, 0)),
                  pl.BlockSpec((1, 1, D), lambda i, j, e: (0, 0, 0))],
        out_specs=xt,
        out_shape=jax.ShapeDtypeStruct((b, t, D), F32),
        scratch_shapes=[pltpu.VMEM((D, tp), F32), pltpu.VMEM((2, NKEYS, tp), F32),
                        pltpu.VMEM((ic * NKEYS, tp), BF)],
        compiler_params=_params(("arbitrary", "arbitrary", "arbitrary")),
        name="peer_main",
    )(h2t, s2, e2, tc, c, u_bf, vt_bf, x1, mod4, ln_g.reshape(1, 1, D), ln_b.reshape(1, 1, D))


def _layer(x, mod, hist, past, w, *, tm, tq, tk, peer_nbt, peer_tt, peer_ic):
    b, t, _ = x.shape
    mod4 = mod.reshape(b, 6, 1, D)
    ya, conv_state = _conv_mixer(x, mod4, hist, w["w3"], w["conv_w"], w["wco"], tm=tm)
    q, k, v, kb, vb, qi, ki, kib, wi = _projections(x, mod4, w["wp"], tm=tm)

    if past is None:
        k_all, v_all, ki_all = kb, vb, kib
        s_len = t
        limits = ((jnp.arange(t // tq, dtype=I32) * tq + (tq - 1)) // CHUNK + 1) * CHUNK
        k_sel = min(TOPK_MAX, t // 4)
    else:
        pk, pv, pki = past
        k_all = jnp.concatenate([pk.reshape(b, -1, NKV * HD).astype(BF), kb], axis=1)
        v_all = jnp.concatenate([pv.reshape(b, -1, NKV * HD).astype(BF), vb], axis=1)
        ki_all = jnp.concatenate([pki.astype(BF), kib], axis=1)
        s_len = k_all.shape[1]
        limits = jnp.full((t // tq,), s_len, I32)
        k_sel = min(TOPK_MAX, s_len // 4)
    s_pad = -(-s_len // tk) * tk
    pad = ((0, 0), (0, s_pad - s_len), (0, 0))
    if s_pad != s_len:
        k_all, v_all, ki_all = (jnp.pad(a, pad) for a in (k_all, v_all, ki_all))
    o = _attention(limits, qi, wi, q, ki_all, k_all, v_all, tq=tq, tk=tk, k_sel=k_sel)

    x1 = _post(x, mod4, ya, o, w["wg"], w["wao"], w["wo"], w["ln1_g"], w["ln1_b"], tm=tm)

    h2t, s2, e2, tc, c = _peer_prep(x1, mod4, w["wqt"], w["gain_b"], w["k1"], w["k2"], nbt=peer_nbt, tt=peer_tt)
    y = _peer_main(h2t, s2, e2, tc, c, w["u"], w["vt"], x1, mod4, w["ln2_g"], w["ln2_b"],
                   nbt=peer_nbt, tt=peer_tt, ic=peer_ic)
    return y, k.reshape(b, t, NKV, HD), v.reshape(b, t, NKV, HD), ki, conv_state


def _prep_weights(w_mix_in, conv_w, w_conv_out, w_attn_out, w_o, ln1_g, ln1_b, w_peer_q, peer_q_gain,
                  sub_keys_1, sub_keys_2, expert_u, expert_v, ln2_g, ln2_b):
    c0 = 3 * DC
    wq = w_mix_in[:, c0:c0 + 2 * _QW + 2 * _KW]
    c1 = c0 + 2 * _QW + 2 * _KW
    wki = jnp.pad(w_mix_in[:, c1:c1 + IDIM], ((0, 0), (0, LANES - IDIM)))
    wwi = jnp.pad(w_mix_in[:, c1 + IDIM:c1 + IDIM + NIH], ((0, 0), (0, LANES - NIH)))
    c2 = c1 + IDIM + NIH
    return dict(
        w3=w_mix_in[:, :c0].astype(BF),
        wp=jnp.concatenate([wq, wki, wwi], axis=1).astype(BF),
        wg=w_mix_in[:, c2:c2 + 2 * D].astype(BF),
        conv_w=conv_w, wco=w_conv_out.astype(BF), wao=w_attn_out.astype(BF), wo=w_o.astype(BF),
        ln1_g=ln1_g, ln1_b=ln1_b,
        wqt=w_peer_q.T.astype(BF),
        gain_b=jnp.broadcast_to(peer_q_gain.reshape(PH * PQ, 1), (PH * PQ, LANES)),
        k1=sub_keys_1.astype(BF), k2=sub_keys_2.astype(BF),
        u=expert_u.astype(BF), vt=expert_v.T.astype(BF),
        ln2_g=ln2_g, ln2_b=ln2_b)


def kernel(x_prompt, x_sample, c_prompt, c_sample, cache_k, cache_v, cache_idx_k, state_conv, w_ada, b_ada,
           w_mix_in, conv_w, w_conv_out, w_attn_out, w_o, ln1_g, ln1_b, w_peer_q, peer_q_gain, sub_keys_1,
           sub_keys_2, expert_u, expert_v, ln2_g, ln2_b):
    bp, bs = x_prompt.shape[0], x_sample.shape[0]
    ts = x_sample.shape[1]
    mod = _modulation(jnp.concatenate([c_prompt, c_sample], axis=0), w_ada[0].astype(BF), b_ada[0])
    w = _prep_weights(w_mix_in[0], conv_w[0], w_conv_out[0], w_attn_out[0], w_o[0], ln1_g[0], ln1_b[0],
                      w_peer_q[0], peer_q_gain[0], sub_keys_1[0], sub_keys_2[0], expert_u[0], expert_v[0],
                      ln2_g[0], ln2_b[0])
    zero_hist = jnp.zeros((bp, 2, DC), F32)
    tmp = min(512, x_prompt.shape[1])
    yp, kp, vp, kip, cp = _layer(x_prompt, mod[:bp], zero_hist, None, w,
                                 tm=tmp, tq=min(4 * CHUNK, x_prompt.shape[1]), tk=512,
                                 peer_nbt=1, peer_tt=tmp, peer_ic=8)
    ys, ks, vs, kis, cs = _layer(x_sample, mod[bp:], state_conv[0], (cache_k[0], cache_v[0], cache_idx_k[0]), w,
                                 tm=ts, tq=ts, tk=512, peer_nbt=bs, peer_tt=ts, peer_ic=8)
    return (yp, ys, kp[None], vp[None], kip[None], cp[None], ks[None], vs[None], kis[None], cs[None])
```

```python
import functools

import jax
import jax.numpy as jnp
from jax import lax
from jax.experimental import pallas as pl
from jax.experimental.pallas import tpu as pltpu

BF = jnp.bfloat16
F32 = jnp.float32
I32 = jnp.int32

D = 1024
DC = 1024
CHUNK = 64
NH = 8
NKV = 2
QPK = NH // NKV
HD = 128
NIH = 16
IDIM = 64
TOPK_MAX = 256
PH = 8
PQ = 256
PHALF = PQ // 2
NKEYS = 128
NEXP = NKEYS * NKEYS
PTOP = 16
LN_EPS = 1e-5
RMS_EPS = 1e-6
ALPHA = 2.0 ** 0.25

LANES = 128
SUBLANES = 8
VMEM_LIMIT = 56 * 1024 * 1024
INT_MIN = -2 ** 31
NEG_BIAS = -3e30
M_INIT = -1e30
LOG2E = 1.4426950408889634

_NT = (((1,), (1,)), ((), ()))


def _params(sem):
    return pltpu.CompilerParams(dimension_semantics=sem, vmem_limit_bytes=VMEM_LIMIT)


def _layer_norm(y, g, b):
    mu = jnp.mean(y, axis=-1, keepdims=True)
    yc = y - mu
    var = jnp.mean(yc * yc, axis=-1, keepdims=True)
    return yc * lax.rsqrt(var + LN_EPS) * g + b


def _mod_kernel(c_ref, w_ref, b_ref, o_ref):
    c = c_ref[...]
    s = c * jax.nn.sigmoid(c)
    o_ref[...] = jnp.dot(s.astype(BF), w_ref[...], preferred_element_type=F32) + b_ref[...]


def _modulation(c, w_ada_bf, b_ada):
    nb = c.shape[0]
    n_out = w_ada_bf.shape[1]
    tn = D
    return pl.pallas_call(
        _mod_kernel,
        grid=(n_out // tn,),
        in_specs=[pl.BlockSpec((nb, D), lambda j: (0, 0)),
                  pl.BlockSpec((D, tn), lambda j: (0, j)),
                  pl.BlockSpec((1, tn), lambda j: (0, j))],
        out_specs=pl.BlockSpec((nb, tn), lambda j: (0, j)),
        out_shape=jax.ShapeDtypeStruct((nb, n_out), F32),
        compiler_params=_params(("arbitrary",)),
        name="modulation",
    )(c, w_ada_bf, b_ada.reshape(1, n_out))


def _mod_spec(k, grid_rank):
    if grid_rank == 2:
        return pl.BlockSpec((None, None, 1, D), lambda b, t: (b, k, 0, 0))
    raise ValueError(grid_rank)


def _conv_kernel(x_ref, sc_ref, sh_ref, hist_ref, w3_ref, cw_ref, wco_ref, ya_ref, cs_ref, ubuf, *, tm):
    t = pl.program_id(1)
    h = (x_ref[...] * (1.0 + sc_ref[...]) + sh_ref[...]).astype(BF)
    z = jnp.dot(h, w3_ref[...], preferred_element_type=F32)
    xin = z[:, :DC]
    gb = z[:, DC:2 * DC]
    gc = z[:, 2 * DC:]

    @pl.when(t == 0)
    def _():
        ubuf[0:SUBLANES, :] = jnp.zeros((SUBLANES, DC), F32)
        ubuf[SUBLANES - 2:SUBLANES, :] = hist_ref[...]

    ubuf[SUBLANES:SUBLANES + tm, :] = gc * xin
    cw = cw_ref[...]
    y = (cw[0:1] * ubuf[SUBLANES - 2:SUBLANES - 2 + tm, :]
         + cw[1:2] * ubuf[SUBLANES - 1:SUBLANES - 1 + tm, :]
         + cw[2:3] * ubuf[SUBLANES:SUBLANES + tm, :])
    ya_ref[...] = jnp.dot((gb * y).astype(BF), wco_ref[...], preferred_element_type=F32)
    tail = ubuf[tm:tm + SUBLANES, :]
    ubuf[0:SUBLANES, :] = tail
    cs_ref[...] = tail[SUBLANES - 2:SUBLANES]


def _conv_mixer(x, mod4, hist, w3_bf, conv_w, wco_bf, *, tm):
    b, t, _ = x.shape
    return pl.pallas_call(
        functools.partial(_conv_kernel, tm=tm),
        grid=(b, t // tm),
        in_specs=[pl.BlockSpec((None, tm, D), lambda i, j: (i, j, 0)),
                  _mod_spec(1, 2), _mod_spec(0, 2),
                  pl.BlockSpec((None, 2, DC), lambda i, j: (i, 0, 0)),
                  pl.BlockSpec((D, 3 * DC), lambda i, j: (0, 0)),
                  pl.BlockSpec((3, DC), lambda i, j: (0, 0)),
                  pl.BlockSpec((DC, D), lambda i, j: (0, 0))],
        out_specs=[pl.BlockSpec((None, tm, D), lambda i, j: (i, j, 0)),
                   pl.BlockSpec((None, 2, DC), lambda i, j: (i, 0, 0))],
        out_shape=[jax.ShapeDtypeStruct((b, t, D), F32),
                   jax.ShapeDtypeStruct((b, 2, DC), F32)],
        scratch_shapes=[pltpu.VMEM((tm + SUBLANES, DC), F32)],
        compiler_params=_params(("arbitrary", "arbitrary")),
        name="conv_mixer",
    )(x, mod4, mod4, hist, w3_bf, conv_w, wco_bf)


_QW = NH * HD
_KW = NKV * HD
_PROJ_COLS = (0, _QW, _QW + _KW, _QW + 2 * _KW, 2 * _QW + 2 * _KW,
              2 * _QW + 2 * _KW + LANES, 2 * _QW + 2 * _KW + 2 * LANES)


def _proj_kernel(x_ref, sc_ref, sh_ref, w_ref, q_ref, k_ref, v_ref, kb_ref, vb_ref, qi_ref, ki_ref, kib_ref, wi_ref):
    h = (x_ref[...] * (1.0 + sc_ref[...]) + sh_ref[...]).astype(BF)
    z = jnp.dot(h, w_ref[...], preferred_element_type=F32)
    c = _PROJ_COLS
    q_ref[...] = z[:, c[0]:c[1]].astype(BF)
    k = z[:, c[1]:c[2]]
    v = z[:, c[2]:c[3]]
    k_ref[...] = k
    v_ref[...] = v
    kb_ref[...] = k.astype(BF)
    vb_ref[...] = v.astype(BF)
    qi_ref[...] = z[:, c[3]:c[4]].astype(BF)
    ki = z[:, c[4]:c[4] + IDIM]
    ki_ref[...] = ki
    kib_ref[...] = ki.astype(BF)
    wi_ref[...] = z[:, c[5]:c[5] + NIH]


def _projections(x, mod4, wp_bf, *, tm):
    b, t, _ = x.shape
    widths = (_QW, _KW, _KW, _KW, _KW, NIH * IDIM, IDIM, IDIM, NIH)
    dtypes = (BF, F32, F32, BF, BF, BF, F32, BF, F32)
    return pl.pallas_call(
        _proj_kernel,
        grid=(b, t // tm),
        in_specs=[pl.BlockSpec((None, tm, D), lambda i, j: (i, j, 0)),
                  _mod_spec(1, 2), _mod_spec(0, 2),
                  pl.BlockSpec(wp_bf.shape, lambda i, j: (0, 0))],
        out_specs=[pl.BlockSpec((None, tm, w), lambda i, j: (i, j, 0)) for w in widths],
        out_shape=[jax.ShapeDtypeStruct((b, t, w), dt) for w, dt in zip(widths, dtypes)],
        compiler_params=_params(("arbitrary", "arbitrary")),
        name="projections",
    )(x, mod4, mod4, wp_bf)


def _attn_kernel(lim_ref, qi_ref, wi_ref, q_ref, ki_ref, k_ref, v_ref, o_ref,
                 key_scr, wb_scr, qi_scr, q_scr, cand_scr, x_scr, bias_scr, lg_scr, p_scr, m_scr, l_scr, acc_scr,
                 *, tq, tk, k_sel):
    n = pl.program_id(1)
    limit = lim_ref[n]
    nt = lax.div(limit + (tk - 1), tk)
    chunk_back = (tq - 1) // CHUNK - lax.broadcasted_iota(I32, (tq, 1), 0) // CHUNK
    lim_rows = limit - CHUNK * chunk_back

    wi = wi_ref[...]
    for h in range(NIH):
        wb_scr[h] = jnp.broadcast_to(wi[:, h:h + 1], (tq, LANES))
    for h in range(NIH):
        qi_scr[h * tq:(h + 1) * tq, :] = qi_ref[:, h * IDIM:(h + 1) * IDIM]
    for h in range(NH):
        q_scr[h * tq:(h + 1) * tq, :] = q_ref[:, h * HD:(h + 1) * HD]
    nlt = tk // LANES
    lane_iota = lax.broadcasted_iota(I32, (tq, LANES), 1)

    def score_tile(t, carry):
        off = pl.multiple_of(t * tk, tk)
        x_scr[...] = lax.dot_general(qi_scr[...], ki_ref[pl.ds(off, tk), :], _NT,
                                     preferred_element_type=F32)
        for c in range(nlt):
            cs = slice(c * LANES, (c + 1) * LANES)
            s = wb_scr[0] * jnp.maximum(x_scr[0:tq, cs], 0.0)
            for h in range(1, NIH):
                s = s + wb_scr[h] * jnp.maximum(x_scr[h * tq:(h + 1) * tq, cs], 0.0)
            bits = pltpu.bitcast(s, I32)
            key = jnp.where(bits < 0, bits ^ jnp.int32(0x7FFFFFFF), bits)
            pos = t * tk + c * LANES + lane_iota
            key_scr[t, :, cs] = jnp.where(pos < lim_rows, key, jnp.int32(INT_MIN))
        return carry

    lax.fori_loop(0, nt, score_tile, 0)

    hq = tq // 2

    def count_ge(cand):
        cand_scr[...] = jnp.broadcast_to(cand, (tq, LANES))

        def body(t, accs):
            out = []
            for half, acc in enumerate(accs):
                rs = slice(half * hq, (half + 1) * hq)
                cand_b = cand_scr[rs, :]
                for c in range(nlt):
                    acc = acc + jnp.where(key_scr[t, rs, c * LANES:(c + 1) * LANES] >= cand_b, 1.0, 0.0)
                out.append(acc)
            return tuple(out)
        zeros = jnp.zeros((hq, LANES), F32)
        accs = lax.fori_loop(0, nt, body, (zeros, zeros))
        return jnp.concatenate([jnp.sum(a, axis=1, keepdims=True) for a in accs], axis=0)

    kf = float(k_sel)
    zero = jnp.zeros((tq, 1), I32)
    res = jnp.where(count_ge(zero) >= kf, zero, jnp.int32(INT_MIN))

    def bit_step(i, res):
        cand = res | lax.shift_left(jnp.int32(1), jnp.int32(30) - i)
        return jnp.where(count_ge(cand) >= kf, cand, res)

    res = lax.fori_loop(0, 31, bit_step, res)
    thr = jnp.maximum(res, jnp.int32(INT_MIN + 1))

    rows = NH * tq
    m_scr[...] = jnp.full((rows, 1), M_INIT, F32)
    l_scr[...] = jnp.zeros((rows, LANES), F32)
    acc_scr[...] = jnp.zeros((rows, HD), F32)
    gr = QPK * tq
    c1 = (HD ** -0.5) * LOG2E
    rc = min(4 * SUBLANES, tq)

    def lane_tiles(x):
        return [x[:, c * LANES:(c + 1) * LANES] for c in range(nlt)]

    def attend_tile(t, carry):
        off = pl.multiple_of(t * tk, tk)
        bias_scr[...] = jnp.where(key_scr[t] >= thr, 0.0, NEG_BIAS)
        for g in range(NKV):
            gs = slice(g * HD, (g + 1) * HD)
            lg_scr[g] = lax.dot_general(q_scr[g * gr:(g + 1) * gr, :], k_ref[pl.ds(off, tk), gs], _NT,
                                        preferred_element_type=F32)
            for sc in range(gr // rc):
                r0 = sc * rc
                b0 = r0 % tq
                rows = slice(g * gr + r0, g * gr + r0 + rc)
                x = lg_scr[g, r0:r0 + rc, :] * c1 + bias_scr[b0:b0 + rc, :]
                m_old = m_scr[rows, :]
                m_new = jnp.maximum(m_old, jnp.max(functools.reduce(jnp.maximum, lane_tiles(x)),
                                                   axis=1, keepdims=True))
                p = jnp.exp2(x - m_new)
                a = jnp.exp2(m_old - m_new)
                l_scr[rows, :] = a * l_scr[rows, :] + functools.reduce(jnp.add, lane_tiles(p))
                m_scr[rows, :] = m_new
                acc_scr[rows, :] = a * acc_scr[rows, :]
                p_scr[g, r0:r0 + rc, :] = p.astype(BF)
            acc_scr[g * gr:(g + 1) * gr, :] += jnp.dot(p_scr[g], v_ref[pl.ds(off, tk), gs],
                                                       preferred_element_type=F32)
        return carry

    lax.fori_loop(0, nt, attend_tile, 0)
    l_tot = jnp.sum(l_scr[...], axis=1, keepdims=True)
    for h in range(NH):
        hs = slice(h * tq, (h + 1) * tq)
        o_ref[:, h * HD:(h + 1) * HD] = (acc_scr[hs, :] / l_tot[hs]).astype(BF)


def _attention(limits, qi, wi, q, ki, k, v, *, tq, tk, k_sel):
    b, t, _ = q.shape
    nblk = t // tq
    s_pad = k.shape[1]
    qblock = lambda w: pl.BlockSpec((None, tq, w), lambda i, j, lim: (i, j, 0))
    keys = lambda w: pl.BlockSpec((None, s_pad, w), lambda i, j, lim: (i, 0, 0), pipeline_mode=pl.Buffered(1))
    grid_spec = pltpu.PrefetchScalarGridSpec(
        num_scalar_prefetch=1,
        grid=(b, nblk),
        in_specs=[qblock(NIH * IDIM), qblock(NIH), qblock(NH * HD),
                  keys(IDIM), keys(NKV * HD), keys(NKV * HD)],
        out_specs=qblock(NH * HD),
        scratch_shapes=[pltpu.VMEM((s_pad // tk, tq, tk), I32),
                        pltpu.VMEM((NIH, tq, LANES), F32),
                        pltpu.VMEM((NIH * tq, IDIM), BF),
                        pltpu.VMEM((NH * tq, HD), BF),
                        pltpu.VMEM((tq, LANES), I32),
                        pltpu.VMEM((NIH * tq, tk), F32),
                        pltpu.VMEM((tq, tk), F32),
                        pltpu.VMEM((NKV, QPK * tq, tk), F32),
                        pltpu.VMEM((NKV, QPK * tq, tk), BF),
                        pltpu.VMEM((NH * tq, 1), F32),
                        pltpu.VMEM((NH * tq, LANES), F32),
                        pltpu.VMEM((NH * tq, HD), F32)])
    return pl.pallas_call(
        functools.partial(_attn_kernel, tq=tq, tk=tk, k_sel=k_sel),
        grid_spec=grid_spec,
        out_shape=jax.ShapeDtypeStruct((b, t, NH * HD), BF),
        compiler_params=_params(("arbitrary", "arbitrary")),
        name="attention",
    )(limits, qi, wi, q, ki, k, v)


def _post_kernel(x_ref, sc_ref, sh_ref, g1_ref, ya_ref, o_ref, wg_ref, wao_ref, wo_ref, lg_ref, lb_ref, out_ref):
    x = x_ref[...]
    h = (x * (1.0 + sc_ref[...]) + sh_ref[...]).astype(BF)
    gates = jnp.dot(h, wg_ref[...], preferred_element_type=F32)
    ga = jax.nn.sigmoid(gates[:, :D])
    gb = jax.nn.sigmoid(gates[:, D:])
    yb = jnp.dot(o_ref[...], wao_ref[...], preferred_element_type=F32)
    merged = ga * ya_ref[...] + gb * yb
    r = jnp.dot(merged.astype(BF), wo_ref[...], preferred_element_type=F32)
    out_ref[...] = _layer_norm(ALPHA * x + g1_ref[...] * r, lg_ref[...], lb_ref[...])


def _post(x, mod4, ya, o, wg_bf, wao_bf, wo_bf, ln_g, ln_b, *, tm):
    b, t, _ = x.shape
    tile = pl.BlockSpec((None, tm, D), lambda i, j: (i, j, 0))
    full = lambda shape: pl.BlockSpec(shape, lambda i, j: (0,) * len(shape))
    return pl.pallas_call(
        _post_kernel,
        grid=(b, t // tm),
        in_specs=[tile, _mod_spec(1, 2), _mod_spec(0, 2), _mod_spec(2, 2), tile, tile,
                  full((D, 2 * D)), full((NH * HD, D)), full((D, D)), full((1, D)), full((1, D))],
        out_specs=tile,
        out_shape=jax.ShapeDtypeStruct((b, t, D), F32),
        compiler_params=_params(("arbitrary", "arbitrary")),
        name="post",
    )(x, mod4, mod4, mod4, ya, o, wg_bf, wao_bf, wo_bf, ln_g.reshape(1, D), ln_b.reshape(1, D))


def _cmp_exchange(v, i, j, descending):
    hi = jnp.maximum(v[i], v[j])
    lo = jnp.minimum(v[i], v[j])
    v[i], v[j] = (hi, lo) if descending else (lo, hi)


def _bitonic_merge(v):
    n = len(v)
    j = n // 2
    while j >= 1:
        for i in range(n):
            l = i ^ j
            if l > i:
                _cmp_exchange(v, i, l, True)
        j //= 2
    return v


def _bitonic_sort(v):
    n = len(v)
    k = 2
    while k <= n:
        j = k // 2
        while j >= 1:
            for i in range(n):
                l = i ^ j
                if l > i:
                    _cmp_exchange(v, i, l, (i & k) == 0)
            j //= 2
        k *= 2
    return v


def _top16(slabs):
    a = _bitonic_sort(list(slabs))
    for shift in (4, 2, 1):
        c = [jnp.maximum(a[i], pltpu.roll(a[PTOP - 1 - i], shift, axis=0)) for i in range(PTOP)]
        a = _bitonic_merge(c)
    return a


def _bf16_pair_words(x):
    hi = pltpu.bitcast(x.astype(BF).astype(F32), I32)
    return hi | lax.shift_right_logical(hi, jnp.int32(16))


def _rows_from_words(word_row):
    packed = pltpu.bitcast(jnp.broadcast_to(word_row, (SUBLANES, LANES)), BF)
    return jnp.concatenate([packed] * (NKEYS // (2 * SUBLANES)), axis=0)


def _peer_prep_kernel(x_ref, sc_ref, sh_ref, wq_ref, gain_ref, k1_ref, k2_ref,
                      h2t_ref, r2_ref, e2_ref, nb_ref, c_ref, *, tp):
    h2 =(x_ref[...] * (1.0 + sc_ref[...]) + sh_ref[...]).reshape(tp, D)
    h2t = h2.T.astype(BF)
    h2t_ref[...] = h2t
    qt = jnp.dot(wq_ref[...], h2t, preferred_element_type=F32)
    sub = lax.broadcasted_iota(I32, (SUBLANES, tp), 0)
    ninf = jnp.float32(-jnp.inf)
    nrep = tp // LANES
    for h in range(PH):
        qh = qt[h * PQ:(h + 1) * PQ, :]
        ms = jnp.mean(qh * qh, axis=0, keepdims=True)
        gain = jnp.concatenate([gain_ref[h * PQ:(h + 1) * PQ, :]] * nrep, axis=1)
        qn = (qh * lax.rsqrt(ms + RMS_EPS) * gain).astype(BF)
        s1 = jnp.dot(k1_ref[h], qn[:PHALF], preferred_element_type=F32).reshape(PTOP, SUBLANES, tp)
        s2 = jnp.dot(k2_ref[h], qn[PHALF:], preferred_element_type=F32).reshape(PTOP, SUBLANES, tp)
        v1 = _top16([s1[i] for i in range(PTOP)])
        v2 = _top16([s2[i] for i in range(PTOP)])
        v1s = v1[SUBLANES - 1]
        for a in range(SUBLANES - 2, -1, -1):
            v1s = jnp.where(sub == a, v1[a], v1s)
        cand = []
        for b in range(PTOP):
            if b < SUBLANES:
                cand.append(jnp.where(sub < PTOP // (b + 1), v1s + v2[b], ninf))
            else:
                cand.append(jnp.where(sub == 0, v1[0] + v2[b], jnp.where(sub == 1, v1[b] + v2[0], ninf)))
        cs = _top16(cand)
        thr = cs[PTOP - 1]
        mx = cs[0]
        z = jnp.exp(cs[0] - mx)
        for k in range(1, PTOP):
            z = z + jnp.exp(cs[k] - mx)
        inv_z = 1.0 / z
        nb = jnp.zeros((PTOP, SUBLANES, tp), F32)
        rank2 = jnp.zeros((PTOP, SUBLANES, tp), F32)
        for b in range(PTOP):
            nb = nb + jnp.where(s1 + v2[b][None] >= thr[None], 1.0, 0.0)
            rank2 = rank2 + jnp.where(v2[b][None] > s2, 1.0, 0.0)
        r2_ref[h] = pltpu.bitcast(rank2.reshape(NKEYS, tp).astype(BF), I32)
        e2_ref[h] = pltpu.bitcast(jnp.exp(s2 - v2[0][None]).reshape(NKEYS, tp).astype(BF), I32)
        nb_ref[h] = _bf16_pair_words(nb.reshape(NKEYS, tp))
        c_ref[h] = _bf16_pair_words((jnp.exp(s1 - v1[0][None]) * inv_z[None]).reshape(NKEYS, tp))


def _peer_prep(x1, mod4, wqt_bf, gain_b, k1_bf, k2_bf, *, nbt, tt):
    b, t, _ = x1.shape
    tp = nbt * tt
    nb_tiles, nt_tiles = b // nbt, t // tt
    ntile = nb_tiles * nt_tiles
    tidx = lambda i, j: i * nt_tiles + j
    big = lambda r: pl.BlockSpec((None, PH, r, tp), lambda i, j: (tidx(i, j), 0, 0, 0))
    mod = lambda k: pl.BlockSpec((nbt, None, 1, D), lambda i, j: (i, k, 0, 0))
    return pl.pallas_call(
        functools.partial(_peer_prep_kernel, tp=tp),
        grid=(nb_tiles, nt_tiles),
        in_specs=[pl.BlockSpec((nbt, tt, D), lambda i, j: (i, j, 0)),
                  mod(4), mod(3),
                  pl.BlockSpec((PH * PQ, D), lambda i, j: (0, 0)),
                  pl.BlockSpec((PH * PQ, LANES), lambda i, j: (0, 0)),
                  pl.BlockSpec((PH, NKEYS, PHALF), lambda i, j: (0, 0, 0)),
                  pl.BlockSpec((PH, NKEYS, PHALF), lambda i, j: (0, 0, 0))],
        out_specs=[pl.BlockSpec((None, D, tp), lambda i, j: (tidx(i, j), 0, 0)),
                   big(NKEYS // 2), big(NKEYS // 2), big(NKEYS), big(NKEYS)],
        out_shape=[jax.ShapeDtypeStruct((ntile, D, tp), BF)]
        + [jax.ShapeDtypeStruct((ntile, PH, r, tp), I32) for r in (NKEYS // 2, NKEYS // 2, NKEYS, NKEYS)],
        compiler_params=_params(("arbitrary", "arbitrary")),
        name="peer_prep",
    )(x1, mod4, mod4, wqt_bf, gain_b, k1_bf, k2_bf)


def _gelu_tanh(a):
    k0 = -2.0 * LOG2E * 0.7978845608028654
    k1 = k0 * 0.044715
    return a / (1.0 + jnp.exp2(a * (k0 + k1 * (a * a))))


def _peer_main_kernel(h2t_ref, r2_ref, e2_ref, nb_ref, c_ref, u_ref, vt_ref, x_ref, g2_ref, lg_ref, lb_ref,
                      out_ref, acc_scr, a_scr, w_scr, *, ic, tp):
    e = pl.program_id(2)
    ne = pl.num_programs(2)

    @pl.when(e == 0)
    def _():
        acc_scr[...] = jnp.zeros_like(acc_scr)

    h2t = h2t_ref[...]

    def expert_inputs(il):
        r0 = pl.multiple_of(jnp.minimum(il, ic - 1) * NKEYS, NKEYS)
        return jnp.dot(u_ref[pl.ds(r0, NKEYS), :], h2t, preferred_element_type=F32)

    def gated(il, a_ref):
        i = e * ic + il
        r0 = pl.multiple_of(il * NKEYS, NKEYS)
        nb_rows = [nb_ref[h, pl.ds(i, 1), :] for h in range(PH)]
        c_rows = [c_ref[h, pl.ds(i, 1), :] for h in range(PH)]
        for c in range(tp // LANES):
            cs = slice(c * LANES, (c + 1) * LANES)
            gate = None
            for h in range(PH):
                sel = jnp.where(pltpu.bitcast(r2_ref[h, :, cs], BF) < _rows_from_words(nb_rows[h][:, cs]),
                                pltpu.bitcast(e2_ref[h, :, cs], BF), jnp.zeros((), BF))
                term = sel * _rows_from_words(c_rows[h][:, cs])
                gate = term if gate is None else gate + term
            act = _gelu_tanh(a_ref[:, cs]).astype(BF)
            w_scr[pl.ds(r0, NKEYS), cs] = act * gate

    a_scr[0] = expert_inputs(0)

    def key_pair(ip, carry):
        a_scr[1] = expert_inputs(2 * ip + 1)
        gated(2 * ip, a_scr.at[0])
        a_scr[0] = expert_inputs(2 * ip + 2)
        gated(2 * ip + 1, a_scr.at[1])
        return carry

    lax.fori_loop(0, ic // 2, key_pair, 0)
    acc_scr[...] += jnp.dot(vt_ref[...], w_scr[...], preferred_element_type=F32)

    @pl.when(e == ne - 1)
    def _():
        ff = acc_scr[...].T.reshape(out_ref.shape)
        out_ref[...] = _layer_norm(ALPHA * x_ref[...] + g2_ref[...] * ff, lg_ref[...], lb_ref[...])


def _peer_main(h2t, s2, e2, tc, c, u_bf, vt_bf, x1, mod4, ln_g, ln_b, *, nbt, tt, ic):
    b, t, _ = x1.shape
    tp = nbt * tt
    nb_tiles, nt_tiles = b // nbt, t // tt
    tidx = lambda i, j: i * nt_tiles + j
    big = lambda r: pl.BlockSpec((None, PH, r, tp), lambda i, j, e: (tidx(i, j), 0, 0, 0))
    xt = pl.BlockSpec((nbt, tt, D), lambda i, j, e: (i, j, 0))
    return pl.pallas_call(
        functools.partial(_peer_main_kernel, ic=ic, tp=tp),
        grid=(nb_tiles, nt_tiles, NKEYS // ic),
        in_specs=[pl.BlockSpec((None, D, tp), lambda i, j, e: (tidx(i, j), 0, 0)),
                  big(NKEYS // 2), big(NKEYS // 2), big(NKEYS), big(NKEYS),
                  pl.BlockSpec((ic * NKEYS, D), lambda i, j, e: (e, 0)),
                  pl.BlockSpec((D, ic * NKEYS), lambda i, j, e: (0, e)),
                  xt,
                  pl.BlockSpec((nbt, None, 1, D), lambda i, j, e: (i, 5, 0, 0)),
                  pl.BlockSpec((1, 1, D), lambda i, j, e: (0, 0, 0)),
                  pl.BlockSpec((1, 1, D), lambda i, j, e: (0, 0, 0))],
        out_specs=xt,
        out_shape=jax.ShapeDtypeStruct((b, t, D), F32),
        scratch_shapes=[pltpu.VMEM((D, tp), F32), pltpu.VMEM((2, NKEYS, tp), F32),
                        pltpu.VMEM((ic * NKEYS, tp), BF)],
        compiler_params=_params(("arbitrary", "arbitrary", "arbitrary")),
        name="peer_main",
    )(h2t, s2, e2, tc, c, u_bf, vt_bf, x1, mod4, ln_g.reshape(1, 1, D), ln_b.reshape(1, 1, D))


def _layer(x, mod, hist, past, w, *, tm, tq, tk, peer_nbt, peer_tt, peer_ic):
    b, t, _ = x.shape
    mod4 = mod.reshape(b, 6, 1, D)
    ya, conv_state = _conv_mixer(x, mod4, hist, w["w3"], w["conv_w"], w["wco"], tm=tm)
    q, k, v, kb, vb, qi, ki, kib, wi = _projections(x, mod4, w["wp"], tm=tm)

    if past is None:
        k_all, v_all, ki_all = kb, vb, kib
        s_len = t
        limits = ((jnp.arange(t // tq, dtype=I32) * tq + (tq - 1)) // CHUNK + 1) * CHUNK
        k_sel = min(TOPK_MAX, t // 4)
    else:
        pk, pv, pki = past
        k_all = jnp.concatenate([pk.reshape(b, -1, NKV * HD).astype(BF), kb], axis=1)
        v_all = jnp.concatenate([pv.reshape(b, -1, NKV * HD).astype(BF), vb], axis=1)
        ki_all = jnp.concatenate([pki.astype(BF), kib], axis=1)
        s_len = k_all.shape[1]
        limits = jnp.full((t // tq,), s_len, I32)
        k_sel = min(TOPK_MAX, s_len // 4)
    s_pad = -(-s_len // tk) * tk
    pad = ((0, 0), (0, s_pad - s_len), (0, 0))
    if s_pad != s_len:
        k_all, v_all, ki_all = (jnp.pad(a, pad) for a in (k_all, v_all, ki_all))
    o = _attention(limits, qi, wi, q, ki_all, k_all, v_all, tq=tq, tk=tk, k_sel=k_sel)

    x1 = _post(x, mod4, ya, o, w["wg"], w["wao"], w["wo"], w["ln1_g"], w["ln1_b"], tm=tm)

    h2t, s2, e2, tc, c = _peer_prep(x1, mod4, w["wqt"], w["gain_b"], w["k1"], w["k2"], nbt=peer_nbt, tt=peer_tt)
    y = _peer_main(h2t, s2, e2, tc, c, w["u"], w["vt"], x1, mod4, w["ln2_g"], w["ln2_b"],
                   nbt=peer_nbt, tt=peer_tt, ic=peer_ic)
    return y, k.reshape(b, t, NKV, HD), v.reshape(b, t, NKV, HD), ki, conv_state


def _prep_weights(w_mix_in, conv_w, w_conv_out, w_attn_out, w_o, ln1_g, ln1_b, w_peer_q, peer_q_gain,
                  sub_keys_1, sub_keys_2, expert_u, expert_v, ln2_g, ln2_b):
    c0 = 3 * DC
    wq = w_mix_in[:, c0:c0 + 2 * _QW + 2 * _KW]
    c1 = c0 + 2 * _QW + 2 * _KW
    wki = jnp.pad(w_mix_in[:, c1:c1 + IDIM], ((0, 0), (0, LANES - IDIM)))
    wwi = jnp.pad(w_mix_in[:, c1 + IDIM:c1 + IDIM + NIH], ((0, 0), (0, LANES - NIH)))
    c2 = c1 + IDIM + NIH
    return dict(
        w3=w_mix_in[:, :c0].astype(BF),
        wp=jnp.concatenate([wq, wki, wwi], axis=1).astype(BF),
        wg=w_mix_in[:, c2:c2 + 2 * D].astype(BF),
        conv_w=conv_w, wco=w_conv_out.astype(BF), wao=w_attn_out.astype(BF), wo=w_o.astype(BF),
        ln1_g=ln1_g, ln1_b=ln1_b,
        wqt=w_peer_q.T.astype(BF),
        gain_b=jnp.broadcast_to(peer_q_gain.reshape(PH * PQ, 1), (PH * PQ, LANES)),
        k1=sub_keys_1.astype(BF), k2=sub_keys_2.astype(BF),
        u=expert_u.astype(BF), vt=expert_v.T.astype(BF),
        ln2_g=ln2_g, ln2_b=ln2_b)


def kernel(x_prompt, x_sample, c_prompt, c_sample, cache_k, cache_v, cache_idx_k, state_conv, w_ada, b_ada,
           w_mix_in, conv_w, w_conv_out, w_attn_out, w_o, ln1_g, ln1_b, w_peer_q, peer_q_gain, sub_keys_1,
           sub_keys_2, expert_u, expert_v, ln2_g, ln2_b):
    bp, bs = x_prompt.shape[0], x_sample.shape[0]
    ts = x_sample.shape[1]
    mod = _modulation(jnp.concatenate([c_prompt, c_sample], axis=0), w_ada[0].astype(BF), b_ada[0])
    w = _prep_weights(w_mix_in[0], conv_w[0], w_conv_out[0], w_attn_out[0], w_o[0], ln1_g[0], ln1_b[0],
                      w_peer_q[0], peer_q_gain[0], sub_keys_1[0], sub_keys_2[0], expert_u[0], expert_v[0],
                      ln2_g[0], ln2_b[0])
    zero_hist = jnp.zeros((bp, 2, DC), F32)
    tmp = min(512, x_prompt.shape[1])
    yp, kp, vp, kip, cp = _layer(x_prompt, mod[:bp], zero_hist, None, w,
                                 tm=tmp, tq=min(4 * CHUNK, x_prompt.shape[1]), tk=512,
                                 peer_nbt=1, peer_tt=tmp, peer_ic=16)
    ys, ks, vs, kis, cs = _layer(x_sample, mod[bp:], state_conv[0], (cache_k[0], cache_v[0], cache_idx_k[0]), w,
                                 tm=ts, tq=ts, tk=512, peer_nbt=bs, peer_tt=ts, peer_ic=8)
    return (yp, ys, kp[None], vp[None], kip[None], cp[None], ks[None], vs[None], kis[None], cs[None])
```

```python
import functools

import jax
import jax.numpy as jnp
from jax import lax
from jax.experimental import pallas as pl
from jax.experimental.pallas import tpu as pltpu

BF = jnp.bfloat16
F32 = jnp.float32
I32 = jnp.int32

D = 1024
DC = 1024
CHUNK = 64
NH = 8
NKV = 2
QPK = NH // NKV
HD = 128
NIH = 16
IDIM = 64
TOPK_MAX = 256
PH = 8
PQ = 256
PHALF = PQ // 2
NKEYS = 128
NEXP = NKEYS * NKEYS
PTOP = 16
LN_EPS = 1e-5
RMS_EPS = 1e-6
ALPHA = 2.0 ** 0.25

LANES = 128
SUBLANES = 8
VMEM_LIMIT = 56 * 1024 * 1024
INT_MIN = -2 ** 31
NEG_BIAS = -3e30
M_INIT = -1e30
LOG2E = 1.4426950408889634

_NT = (((1,), (1,)), ((), ()))


def _params(sem):
    return pltpu.CompilerParams(dimension_semantics=sem, vmem_limit_bytes=VMEM_LIMIT)


def _layer_norm(y, g, b):
    mu = jnp.mean(y, axis=-1, keepdims=True)
    yc = y - mu
    var = jnp.mean(yc * yc, axis=-1, keepdims=True)
    return yc * lax.rsqrt(var + LN_EPS) * g + b


def _mod_kernel(c_ref, w_ref, b_ref, o_ref):
    c = c_ref[...]
    s = c * jax.nn.sigmoid(c)
    o_ref[...] = jnp.dot(s.astype(BF), w_ref[...], preferred_element_type=F32) + b_ref[...]


def _modulation(c, w_ada_bf, b_ada):
    nb = c.shape[0]
    n_out = w_ada_bf.shape[1]
    tn = D
    return pl.pallas_call(
        _mod_kernel,
        grid=(n_out // tn,),
        in_specs=[pl.BlockSpec((nb, D), lambda j: (0, 0)),
                  pl.BlockSpec((D, tn), lambda j: (0, j)),
                  pl.BlockSpec((1, tn), lambda j: (0, j))],
        out_specs=pl.BlockSpec((nb, tn), lambda j: (0, j)),
        out_shape=jax.ShapeDtypeStruct((nb, n_out), F32),
        compiler_params=_params(("arbitrary",)),
        name="modulation",
    )(c, w_ada_bf, b_ada.reshape(1, n_out))


def _mod_spec(k, grid_rank):
    if grid_rank == 2:
        return pl.BlockSpec((None, None, 1, D), lambda b, t: (b, k, 0, 0))
    raise ValueError(grid_rank)


def _conv_kernel(x_ref, sc_ref, sh_ref, hist_ref, w3_ref, cw_ref, wco_ref, ya_ref, cs_ref, ubuf, *, tm):
    t = pl.program_id(1)
    h = (x_ref[...] * (1.0 + sc_ref[...]) + sh_ref[...]).astype(BF)
    z = jnp.dot(h, w3_ref[...], preferred_element_type=F32)
    xin = z[:, :DC]
    gb = z[:, DC:2 * DC]
    gc = z[:, 2 * DC:]

    @pl.when(t == 0)
    def _():
        ubuf[0:SUBLANES, :] = jnp.zeros((SUBLANES, DC), F32)
        ubuf[SUBLANES - 2:SUBLANES, :] = hist_ref[...]

    ubuf[SUBLANES:SUBLANES + tm, :] = gc * xin
    cw = cw_ref[...]
    y = (cw[0:1] * ubuf[SUBLANES - 2:SUBLANES - 2 + tm, :]
         + cw[1:2] * ubuf[SUBLANES - 1:SUBLANES - 1 + tm, :]
         + cw[2:3] * ubuf[SUBLANES:SUBLANES + tm, :])
    ya_ref[...] = jnp.dot((gb * y).astype(BF), wco_ref[...], preferred_element_type=F32)
    tail = ubuf[tm:tm + SUBLANES, :]
    ubuf[0:SUBLANES, :] = tail
    cs_ref[...] = tail[SUBLANES - 2:SUBLANES]


def _conv_mixer(x, mod4, hist, w3_bf, conv_w, wco_bf, *, tm):
    b, t, _ = x.shape
    return pl.pallas_call(
        functools.partial(_conv_kernel, tm=tm),
        grid=(b, t // tm),
        in_specs=[pl.BlockSpec((None, tm, D), lambda i, j: (i, j, 0)),
                  _mod_spec(1, 2), _mod_spec(0, 2),
                  pl.BlockSpec((None, 2, DC), lambda i, j: (i, 0, 0)),
                  pl.BlockSpec((D, 3 * DC), lambda i, j: (0, 0)),
                  pl.BlockSpec((3, DC), lambda i, j: (0, 0)),
                  pl.BlockSpec((DC, D), lambda i, j: (0, 0))],
        out_specs=[pl.BlockSpec((None, tm, D), lambda i, j: (i, j, 0)),
                   pl.BlockSpec((None, 2, DC), lambda i, j: (i, 0, 0))],
        out_shape=[jax.ShapeDtypeStruct((b, t, D), F32),
                   jax.ShapeDtypeStruct((b, 2, DC), F32)],
        scratch_shapes=[pltpu.VMEM((tm + SUBLANES, DC), F32)],
        compiler_params=_params(("arbitrary", "arbitrary")),
        name="conv_mixer",
    )(x, mod4, mod4, hist, w3_bf, conv_w, wco_bf)


_QW = NH * HD
_KW = NKV * HD
_PROJ_COLS = (0, _QW, _QW + _KW, _QW + 2 * _KW, 2 * _QW + 2 * _KW,
              2 * _QW + 2 * _KW + LANES, 2 * _QW + 2 * _KW + 2 * LANES)


def _proj_kernel(x_ref, sc_ref, sh_ref, w_ref, q_ref, k_ref, v_ref, kb_ref, vb_ref, qi_ref, ki_ref, kib_ref, wi_ref):
    h = (x_ref[...] * (1.0 + sc_ref[...]) + sh_ref[...]).astype(BF)
    z = jnp.dot(h, w_ref[...], preferred_element_type=F32)
    c = _PROJ_COLS
    q_ref[...] = z[:, c[0]:c[1]].astype(BF)
    k = z[:, c[1]:c[2]]
    v = z[:, c[2]:c[3]]
    k_ref[...] = k
    v_ref[...] = v
    kb_ref[...] = k.astype(BF)
    vb_ref[...] = v.astype(BF)
    qi_ref[...] = z[:, c[3]:c[4]].astype(BF)
    ki = z[:, c[4]:c[4] + IDIM]
    ki_ref[...] = ki
    kib_ref[...] = ki.astype(BF)
    wi_ref[...] = z[:, c[5]:c[5] + NIH]


def _projections(x, mod4, wp_bf, *, tm):
    b, t, _ = x.shape
    widths = (_QW, _KW, _KW, _KW, _KW, NIH * IDIM, IDIM, IDIM, NIH)
    dtypes = (BF, F32, F32, BF, BF, BF, F32, BF, F32)
    return pl.pallas_call(
        _proj_kernel,
        grid=(b, t // tm),
        in_specs=[pl.BlockSpec((None, tm, D), lambda i, j: (i, j, 0)),
                  _mod_spec(1, 2), _mod_spec(0, 2),
                  pl.BlockSpec(wp_bf.shape, lambda i, j: (0, 0))],
        out_specs=[pl.BlockSpec((None, tm, w), lambda i, j: (i, j, 0)) for w in widths],
        out_shape=[jax.ShapeDtypeStruct((b, t, w), dt) for w, dt in zip(widths, dtypes)],
        compiler_params=_params(("arbitrary", "arbitrary")),
        name="projections",
    )(x, mod4, mod4, wp_bf)


def _attn_kernel(lim_ref, qi_ref, wi_ref, q_ref, ki_ref, k_ref, v_ref, o_ref,
                 key_scr, wb_scr, qi_scr, q_scr, cand_scr, x_scr, bias_scr, lg_scr, p_scr, m_scr, l_scr, acc_scr,
                 *, tq, tk, k_sel):
    n = pl.program_id(1)
    limit = lim_ref[n]
    nt = lax.div(limit + (tk - 1), tk)
    chunk_back = (tq - 1) // CHUNK - lax.broadcasted_iota(I32, (tq, 1), 0) // CHUNK
    lim_rows = limit - CHUNK * chunk_back

    wi = wi_ref[...]
    for h in range(NIH):
        wb_scr[h] = jnp.broadcast_to(wi[:, h:h + 1], (tq, LANES))
    for h in range(NIH):
        qi_scr[h * tq:(h + 1) * tq, :] = qi_ref[:, h * IDIM:(h + 1) * IDIM]
    for h in range(NH):
        q_scr[h * tq:(h + 1) * tq, :] = q_ref[:, h * HD:(h + 1) * HD]
    nlt = tk // LANES
    lane_iota = lax.broadcasted_iota(I32, (tq, LANES), 1)

    def score_tile(t, carry):
        off = pl.multiple_of(t * tk, tk)
        x_scr[...] = lax.dot_general(qi_scr[...], ki_ref[pl.ds(off, tk), :], _NT,
                                     preferred_element_type=F32)
        for c in range(nlt):
            cs = slice(c * LANES, (c + 1) * LANES)
            s = wb_scr[0] * jnp.maximum(x_scr[0:tq, cs], 0.0)
            for h in range(1, NIH):
                s = s + wb_scr[h] * jnp.maximum(x_scr[h * tq:(h + 1) * tq, cs], 0.0)
            bits = pltpu.bitcast(s, I32)
            key = jnp.where(bits < 0, bits ^ jnp.int32(0x7FFFFFFF), bits)
            pos = t * tk + c * LANES + lane_iota
            key_scr[t, :, cs] = jnp.where(pos < lim_rows, key, jnp.int32(INT_MIN))
        return carry

    lax.fori_loop(0, nt, score_tile, 0)

    hq = tq // 2

    def count_ge(cand):
        cand_scr[...] = cand

        def body(t, accs):
            out = []
            for half, acc in enumerate(accs):
                rs = slice(half * hq, (half + 1) * hq)
                cand_b = cand_scr[rs, :]
                for c in range(nlt):
                    acc = acc + jnp.where(key_scr[t, rs, c * LANES:(c + 1) * LANES] >= cand_b, 1.0, 0.0)
                out.append(acc)
            return tuple(out)
        zeros = jnp.zeros((hq, LANES), F32)
        accs = lax.fori_loop(0, nt, body, (zeros, zeros))
        part = jnp.concatenate(accs, axis=0).astype(BF)
        return jnp.dot(part, jnp.ones((LANES, LANES), BF), preferred_element_type=F32)

    kf = float(k_sel)
    zero = jnp.zeros((tq, LANES), I32)
    res = jnp.where(count_ge(zero) >= kf, zero, jnp.int32(INT_MIN))

    def bit_step(i, res):
        cand = res | lax.shift_left(jnp.int32(1), jnp.int32(30) - i)
        return jnp.where(count_ge(cand) >= kf, cand, res)

    res = lax.fori_loop(0, 31, bit_step, res)
    thr = jnp.maximum(res[:, 0:1], jnp.int32(INT_MIN + 1))

    rows = NH * tq
    m_scr[...] = jnp.full((rows, 1), M_INIT, F32)
    l_scr[...] = jnp.zeros((rows, LANES), F32)
    acc_scr[...] = jnp.zeros((rows, HD), F32)
    gr = QPK * tq
    c1 = (HD ** -0.5) * LOG2E
    rc = min(4 * SUBLANES, tq)

    def lane_tiles(x):
        return [x[:, c * LANES:(c + 1) * LANES] for c in range(nlt)]

    def attend_tile(t, carry):
        off = pl.multiple_of(t * tk, tk)
        bias_scr[...] = jnp.where(key_scr[t] >= thr, 0.0, NEG_BIAS)
        for g in range(NKV):
            gs = slice(g * HD, (g + 1) * HD)
            lg_scr[g] = lax.dot_general(q_scr[g * gr:(g + 1) * gr, :], k_ref[pl.ds(off, tk), gs], _NT,
                                        preferred_element_type=F32)
            for sc in range(gr // rc):
                r0 = sc * rc
                b0 = r0 % tq
                rows = slice(g * gr + r0, g * gr + r0 + rc)
                x = lg_scr[g, r0:r0 + rc, :] * c1 + bias_scr[b0:b0 + rc, :]
                m_old = m_scr[rows, :]
                m_new = jnp.maximum(m_old, jnp.max(functools.reduce(jnp.maximum, lane_tiles(x)),
                                                   axis=1, keepdims=True))
                p = jnp.exp2(x - m_new)
                a = jnp.exp2(m_old - m_new)
                l_scr[rows, :] = a * l_scr[rows, :] + functools.reduce(jnp.add, lane_tiles(p))
                m_scr[rows, :] = m_new
                acc_scr[rows, :] = a * acc_scr[rows, :]
                p_scr[g, r0:r0 + rc, :] = p.astype(BF)
            acc_scr[g * gr:(g + 1) * gr, :] += jnp.dot(p_scr[g], v_ref[pl.ds(off, tk), gs],
                                                       preferred_element_type=F32)
        return carry

    lax.fori_loop(0, nt, attend_tile, 0)
    l_tot = jnp.sum(l_scr[...], axis=1, keepdims=True)
    for h in range(NH):
        hs = slice(h * tq, (h + 1) * tq)
        o_ref[:, h * HD:(h + 1) * HD] = (acc_scr[hs, :] / l_tot[hs]).astype(BF)


def _attention(limits, qi, wi, q, ki, k, v, *, tq, tk, k_sel):
    b, t, _ = q.shape
    nblk = t // tq
    s_pad = k.shape[1]
    qblock = lambda w: pl.BlockSpec((None, tq, w), lambda i, j, lim: (i, j, 0))
    keys = lambda w: pl.BlockSpec((None, s_pad, w), lambda i, j, lim: (i, 0, 0), pipeline_mode=pl.Buffered(1))
    grid_spec = pltpu.PrefetchScalarGridSpec(
        num_scalar_prefetch=1,
        grid=(b, nblk),
        in_specs=[qblock(NIH * IDIM), qblock(NIH), qblock(NH * HD),
                  keys(IDIM), keys(NKV * HD), keys(NKV * HD)],
        out_specs=qblock(NH * HD),
        scratch_shapes=[pltpu.VMEM((s_pad // tk, tq, tk), I32),
                        pltpu.VMEM((NIH, tq, LANES), F32),
                        pltpu.VMEM((NIH * tq, IDIM), BF),
                        pltpu.VMEM((NH * tq, HD), BF),
                        pltpu.VMEM((tq, LANES), I32),
                        pltpu.VMEM((NIH * tq, tk), F32),
                        pltpu.VMEM((tq, tk), F32),
                        pltpu.VMEM((NKV, QPK * tq, tk), F32),
                        pltpu.VMEM((NKV, QPK * tq, tk), BF),
                        pltpu.VMEM((NH * tq, 1), F32),
                        pltpu.VMEM((NH * tq, LANES), F32),
                        pltpu.VMEM((NH * tq, HD), F32)])
    return pl.pallas_call(
        functools.partial(_attn_kernel, tq=tq, tk=tk, k_sel=k_sel),
        grid_spec=grid_spec,
        out_shape=jax.ShapeDtypeStruct((b, t, NH * HD), BF),
        compiler_params=_params(("arbitrary", "arbitrary")),
        name="attention",
    )(limits, qi, wi, q, ki, k, v)


def _post_kernel(x_ref, sc_ref, sh_ref, g1_ref, ya_ref, o_ref, wg_ref, wao_ref, wo_ref, lg_ref, lb_ref, out_ref):
    x = x_ref[...]
    h = (x * (1.0 + sc_ref[...]) + sh_ref[...]).astype(BF)
    gates = jnp.dot(h, wg_ref[...], preferred_element_type=F32)
    ga = jax.nn.sigmoid(gates[:, :D])
    gb = jax.nn.sigmoid(gates[:, D:])
    yb = jnp.dot(o_ref[...], wao_ref[...], preferred_element_type=F32)
    merged = ga * ya_ref[...] + gb * yb
    r = jnp.dot(merged.astype(BF), wo_ref[...], preferred_element_type=F32)
    out_ref[...] = _layer_norm(ALPHA * x + g1_ref[...] * r, lg_ref[...], lb_ref[...])


def _post(x, mod4, ya, o, wg_bf, wao_bf, wo_bf, ln_g, ln_b, *, tm):
    b, t, _ = x.shape
    tile = pl.BlockSpec((None, tm, D), lambda i, j: (i, j, 0))
    full = lambda shape: pl.BlockSpec(shape, lambda i, j: (0,) * len(shape))
    return pl.pallas_call(
        _post_kernel,
        grid=(b, t // tm),
        in_specs=[tile, _mod_spec(1, 2), _mod_spec(0, 2), _mod_spec(2, 2), tile, tile,
                  full((D, 2 * D)), full((NH * HD, D)), full((D, D)), full((1, D)), full((1, D))],
        out_specs=tile,
        out_shape=jax.ShapeDtypeStruct((b, t, D), F32),
        compiler_params=_params(("arbitrary", "arbitrary")),
        name="post",
    )(x, mod4, mod4, mod4, ya, o, wg_bf, wao_bf, wo_bf, ln_g.reshape(1, D), ln_b.reshape(1, D))


def _cmp_exchange(v, i, j, descending):
    hi = jnp.maximum(v[i], v[j])
    lo = jnp.minimum(v[i], v[j])
    v[i], v[j] = (hi, lo) if descending else (lo, hi)


def _bitonic_merge(v):
    n = len(v)
    j = n // 2
    while j >= 1:
        for i in range(n):
            l = i ^ j
            if l > i:
                _cmp_exchange(v, i, l, True)
        j //= 2
    return v


def _bitonic_sort(v):
    n = len(v)
    k = 2
    while k <= n:
        j = k // 2
        while j >= 1:
            for i in range(n):
                l = i ^ j
                if l > i:
                    _cmp_exchange(v, i, l, (i & k) == 0)
            j //= 2
        k *= 2
    return v


def _top16(slabs):
    a = _bitonic_sort(list(slabs))
    for shift in (4, 2, 1):
        c = [jnp.maximum(a[i], pltpu.roll(a[PTOP - 1 - i], shift, axis=0)) for i in range(PTOP)]
        a = _bitonic_merge(c)
    return a


def _bf16_pair_words(x):
    hi = pltpu.bitcast(x.astype(BF).astype(F32), I32)
    return hi | lax.shift_right_logical(hi, jnp.int32(16))


def _rows_from_words(word_row):
    packed = pltpu.bitcast(jnp.broadcast_to(word_row, (SUBLANES, LANES)), BF)
    return jnp.concatenate([packed] * (NKEYS // (2 * SUBLANES)), axis=0)


def _peer_prep_kernel(x_ref, sc_ref, sh_ref, wq_ref, gain_ref, k1_ref, k2_ref,
                      h2t_ref, r2_ref, e2_ref, nb_ref, c_ref, *, tp):
    h2 =(x_ref[...] * (1.0 + sc_ref[...]) + sh_ref[...]).reshape(tp, D)
    h2t = h2.T.astype(BF)
    h2t_ref[...] = h2t
    qt = jnp.dot(wq_ref[...], h2t, preferred_element_type=F32)
    sub = lax.broadcasted_iota(I32, (SUBLANES, tp), 0)
    ninf = jnp.float32(-jnp.inf)
    nrep = tp // LANES
    for h in range(PH):
        qh = qt[h * PQ:(h + 1) * PQ, :]
        ms = jnp.mean(qh * qh, axis=0, keepdims=True)
        gain = jnp.concatenate([gain_ref[h * PQ:(h + 1) * PQ, :]] * nrep, axis=1)
        qn = (qh * lax.rsqrt(ms + RMS_EPS) * gain).astype(BF)
        s1 = jnp.dot(k1_ref[h], qn[:PHALF], preferred_element_type=F32).reshape(PTOP, SUBLANES, tp)
        s2 = jnp.dot(k2_ref[h], qn[PHALF:], preferred_element_type=F32).reshape(PTOP, SUBLANES, tp)
        v1 = _top16([s1[i] for i in range(PTOP)])
        v2 = _top16([s2[i] for i in range(PTOP)])
        v1s = v1[SUBLANES - 1]
        for a in range(SUBLANES - 2, -1, -1):
            v1s = jnp.where(sub == a, v1[a], v1s)
        cand = []
        for b in range(PTOP):
            if b < SUBLANES:
                cand.append(jnp.where(sub < PTOP // (b + 1), v1s + v2[b], ninf))
            else:
                cand.append(jnp.where(sub == 0, v1[0] + v2[b], jnp.where(sub == 1, v1[b] + v2[0], ninf)))
        cs = _top16(cand)
        thr = cs[PTOP - 1]
        mx = cs[0]
        z = jnp.exp(cs[0] - mx)
        for k in range(1, PTOP):
            z = z + jnp.exp(cs[k] - mx)
        inv_z = 1.0 / z
        nb = jnp.zeros((PTOP, SUBLANES, tp), F32)
        rank2 = jnp.zeros((PTOP, SUBLANES, tp), F32)
        for b in range(PTOP):
            nb = nb + jnp.where(s1 + v2[b][None] >= thr[None], 1.0, 0.0)
            rank2 = rank2 + jnp.where(v2[b][None] > s2, 1.0, 0.0)
        r2_ref[h] = pltpu.bitcast(rank2.reshape(NKEYS, tp).astype(BF), I32)
        e2_ref[h] = pltpu.bitcast(jnp.exp(s2 - v2[0][None]).reshape(NKEYS, tp).astype(BF), I32)
        nb_ref[h] = _bf16_pair_words(nb.reshape(NKEYS, tp))
        c_ref[h] = _bf16_pair_words((jnp.exp(s1 - v1[0][None]) * inv_z[None]).reshape(NKEYS, tp))


def _peer_prep(x1, mod4, wqt_bf, gain_b, k1_bf, k2_bf, *, nbt, tt):
    b, t, _ = x1.shape
    tp = nbt * tt
    nb_tiles, nt_tiles = b // nbt, t // tt
    ntile = nb_tiles * nt_tiles
    tidx = lambda i, j: i * nt_tiles + j
    big = lambda r: pl.BlockSpec((None, PH, r, tp), lambda i, j: (tidx(i, j), 0, 0, 0))
    mod = lambda k: pl.BlockSpec((nbt, None, 1, D), lambda i, j: (i, k, 0, 0))
    return pl.pallas_call(
        functools.partial(_peer_prep_kernel, tp=tp),
        grid=(nb_tiles, nt_tiles),
        in_specs=[pl.BlockSpec((nbt, tt, D), lambda i, j: (i, j, 0)),
                  mod(4), mod(3),
                  pl.BlockSpec((PH * PQ, D), lambda i, j: (0, 0)),
                  pl.BlockSpec((PH * PQ, LANES), lambda i, j: (0, 0)),
                  pl.BlockSpec((PH, NKEYS, PHALF), lambda i, j: (0, 0, 0)),
                  pl.BlockSpec((PH, NKEYS, PHALF), lambda i, j: (0, 0, 0))],
        out_specs=[pl.BlockSpec((None, D, tp), lambda i, j: (tidx(i, j), 0, 0)),
                   big(NKEYS // 2), big(NKEYS // 2), big(NKEYS), big(NKEYS)],
        out_shape=[jax.ShapeDtypeStruct((ntile, D, tp), BF)]
        + [jax.ShapeDtypeStruct((ntile, PH, r, tp), I32) for r in (NKEYS // 2, NKEYS // 2, NKEYS, NKEYS)],
        compiler_params=_params(("arbitrary", "arbitrary")),
        name="peer_prep",
    )(x1, mod4, mod4, wqt_bf, gain_b, k1_bf, k2_bf)


def _gelu_tanh(a):
    k0 = -2.0 * LOG2E * 0.7978845608028654
    k1 = k0 * 0.044715
    return a / (1.0 + jnp.exp2(a * (k0 + k1 * (a * a))))


def _peer_main_kernel(h2t_ref, r2_ref, e2_ref, nb_ref, c_ref, u_ref, vt_ref, x_ref, g2_ref, lg_ref, lb_ref,
                      out_ref, acc_scr, a_scr, w_scr, *, ic, tp):
    e = pl.program_id(2)
    ne = pl.num_programs(2)

    @pl.when(e == 0)
    def _():
        acc_scr[...] = jnp.zeros_like(acc_scr)

    h2t = h2t_ref[...]

    def expert_inputs(il):
        r0 = pl.multiple_of(jnp.minimum(il, ic - 1) * NKEYS, NKEYS)
        return jnp.dot(u_ref[pl.ds(r0, NKEYS), :], h2t, preferred_element_type=F32)

    def gated(il, a_ref):
        i = e * ic + il
        r0 = pl.multiple_of(il * NKEYS, NKEYS)
        nb_rows = [nb_ref[h, pl.ds(i, 1), :] for h in range(PH)]
        c_rows = [c_ref[h, pl.ds(i, 1), :] for h in range(PH)]
        for c in range(tp // LANES):
            cs = slice(c * LANES, (c + 1) * LANES)
            gate = None
            for h in range(PH):
                sel = jnp.where(pltpu.bitcast(r2_ref[h, :, cs], BF) < _rows_from_words(nb_rows[h][:, cs]),
                                pltpu.bitcast(e2_ref[h, :, cs], BF), jnp.zeros((), BF))
                term = sel * _rows_from_words(c_rows[h][:, cs])
                gate = term if gate is None else gate + term
            act = _gelu_tanh(a_ref[:, cs]).astype(BF)
            w_scr[pl.ds(r0, NKEYS), cs] = act * gate

    a_scr[0] = expert_inputs(0)

    def key_pair(ip, carry):
        a_scr[1] = expert_inputs(2 * ip + 1)
        gated(2 * ip, a_scr.at[0])
        a_scr[0] = expert_inputs(2 * ip + 2)
        gated(2 * ip + 1, a_scr.at[1])
        return carry

    lax.fori_loop(0, ic // 2, key_pair, 0)
    acc_scr[...] += jnp.dot(vt_ref[...], w_scr[...], preferred_element_type=F32)

    @pl.when(e == ne - 1)
    def _():
        ff = acc_scr[...].T.reshape(out_ref.shape)
        out_ref[...] = _layer_norm(ALPHA * x_ref[...] + g2_ref[...] * ff, lg_ref[...], lb_ref[...])


def _peer_main(h2t, s2, e2, tc, c, u_bf, vt_bf, x1, mod4, ln_g, ln_b, *, nbt, tt, ic):
    b, t, _ = x1.shape
    tp = nbt * tt
    nb_tiles, nt_tiles = b // nbt, t // tt
    tidx = lambda i, j: i * nt_tiles + j
    big = lambda r: pl.BlockSpec((None, PH, r, tp), lambda i, j, e: (tidx(i, j), 0, 0, 0))
    xt = pl.BlockSpec((nbt, tt, D), lambda i, j, e: (i, j, 0))
    return pl.pallas_call(
        functools.partial(_peer_main_kernel, ic=ic, tp=tp),
        grid=(nb_tiles, nt_tiles, NKEYS // ic),
        in_specs=[pl.BlockSpec((None, D, tp), lambda i, j, e: (tidx(i, j), 0, 0)),
                  big(NKEYS // 2), big(NKEYS // 2), big(NKEYS), big(NKEYS),
                  pl.BlockSpec((ic * NKEYS, D), lambda i, j, e: (e, 0)),
                  pl.BlockSpec((D, ic * NKEYS), lambda i, j, e: (0, e)),
                  xt,
                  pl.BlockSpec((nbt, None, 1, D), lambda i, j, e: (i, 5, 0, 0)),
                  pl.BlockSpec((1, 1, D), lambda i, j, e: (0, 0, 0)),
                  pl.BlockSpec((1, 1, D), lambda i, j, e: (0, 0, 0))],
        out_specs=xt,
        out_shape=jax.ShapeDtypeStruct((b, t, D), F32),
        scratch_shapes=[pltpu.VMEM((D, tp), F32), pltpu.VMEM((2, NKEYS, tp), F32),
                        pltpu.VMEM((ic * NKEYS, tp), BF)],
        compiler_params=_params(("arbitrary", "arbitrary", "arbitrary")),
        name="peer_main",
    )(h2t, s2, e2, tc, c, u_bf, vt_bf, x1, mod4, ln_g.reshape(1, 1, D), ln_b.reshape(1, 1, D))


def _layer(x, mod, hist, past, w, *, tm, tq, tk, peer_nbt, peer_tt, peer_ic):
    b, t, _ = x.shape
    mod4 = mod.reshape(b, 6, 1, D)
    ya, conv_state = _conv_mixer(x, mod4, hist, w["w3"], w["conv_w"], w["wco"], tm=tm)
    q, k, v, kb, vb, qi, ki, kib, wi = _projections(x, mod4, w["wp"], tm=tm)

    if past is None:
        k_all, v_all, ki_all = kb, vb, kib
        s_len = t
        limits = ((jnp.arange(t // tq, dtype=I32) * tq + (tq - 1)) // CHUNK + 1) * CHUNK
        k_sel = min(TOPK_MAX, t // 4)
    else:
        pk, pv, pki = past
        k_all = jnp.concatenate([pk.reshape(b, -1, NKV * HD).astype(BF), kb], axis=1)
        v_all = jnp.concatenate([pv.reshape(b, -1, NKV * HD).astype(BF), vb], axis=1)
        ki_all = jnp.concatenate([pki.astype(BF), kib], axis=1)
        s_len = k_all.shape[1]
        limits = jnp.full((t // tq,), s_len, I32)
        k_sel = min(TOPK_MAX, s_len // 4)
    s_pad = -(-s_len // tk) * tk
    pad = ((0, 0), (0, s_pad - s_len), (0, 0))
    if s_pad != s_len:
        k_all, v_all, ki_all = (jnp.pad(a, pad) for a in (k_all, v_all, ki_all))
    o = _attention(limits, qi, wi, q, ki_all, k_all, v_all, tq=tq, tk=tk, k_sel=k_sel)

    x1 = _post(x, mod4, ya, o, w["wg"], w["wao"], w["wo"], w["ln1_g"], w["ln1_b"], tm=tm)

    h2t, s2, e2, tc, c = _peer_prep(x1, mod4, w["wqt"], w["gain_b"], w["k1"], w["k2"], nbt=peer_nbt, tt=peer_tt)
    y = _peer_main(h2t, s2, e2, tc, c, w["u"], w["vt"], x1, mod4, w["ln2_g"], w["ln2_b"],
                   nbt=peer_nbt, tt=peer_tt, ic=peer_ic)
    return y, k.reshape(b, t, NKV, HD), v.reshape(b, t, NKV, HD), ki, conv_state


def _prep_weights(w_mix_in, conv_w, w_conv_out, w_attn_out, w_o, ln1_g, ln1_b, w_peer_q, peer_q_gain,
                  sub_keys_1, sub_keys_2, expert_u, expert_v, ln2_g, ln2_b):
    c0 = 3 * DC
    wq = w_mix_in[:, c0:c0 + 2 * _QW + 2 * _KW]
    c1 = c0 + 2 * _QW + 2 * _KW
    wki = jnp.pad(w_mix_in[:, c1:c1 + IDIM], ((0, 0), (0, LANES - IDIM)))
    wwi = jnp.pad(w_mix_in[:, c1 + IDIM:c1 + IDIM + NIH], ((0, 0), (0, LANES - NIH)))
    c2 = c1 + IDIM + NIH
    return dict(
        w3=w_mix_in[:, :c0].astype(BF),
        wp=jnp.concatenate([wq, wki, wwi], axis=1).astype(BF),
        wg=w_mix_in[:, c2:c2 + 2 * D].astype(BF),
        conv_w=conv_w, wco=w_conv_out.astype(BF), wao=w_attn_out.astype(BF), wo=w_o.astype(BF),
        ln1_g=ln1_g, ln1_b=ln1_b,
        wqt=w_peer_q.T.astype(BF),
        gain_b=jnp.broadcast_to(peer_q_gain.reshape(PH * PQ, 1), (PH * PQ, LANES)),
        k1=sub_keys_1.astype(BF), k2=sub_keys_2.astype(BF),
        u=expert_u.astype(BF), vt=expert_v.T.astype(BF),
        ln2_g=ln2_g, ln2_b=ln2_b)


def kernel(x_prompt, x_sample, c_prompt, c_sample, cache_k, cache_v, cache_idx_k, state_conv, w_ada, b_ada,
           w_mix_in, conv_w, w_conv_out, w_attn_out, w_o, ln1_g, ln1_b, w_peer_q, peer_q_gain, sub_keys_1,
           sub_keys_2, expert_u, expert_v, ln2_g, ln2_b):
    bp, bs = x_prompt.shape[0], x_sample.shape[0]
    ts = x_sample.shape[1]
    mod = _modulation(jnp.concatenate([c_prompt, c_sample], axis=0), w_ada[0].astype(BF), b_ada[0])
    w = _prep_weights(w_mix_in[0], conv_w[0], w_conv_out[0], w_attn_out[0], w_o[0], ln1_g[0], ln1_b[0],
                      w_peer_q[0], peer_q_gain[0], sub_keys_1[0], sub_keys_2[0], expert_u[0], expert_v[0],
                      ln2_g[0], ln2_b[0])
    zero_hist = jnp.zeros((bp, 2, DC), F32)
    tmp = min(512, x_prompt.shape[1])
    yp, kp, vp, kip, cp = _layer(x_prompt, mod[:bp], zero_hist, None, w,
                                 tm=tmp, tq=min(4 * CHUNK, x_prompt.shape[1]), tk=512,
                                 peer_nbt=1, peer_tt=tmp, peer_ic=16)
    ys, ks, vs, kis, cs = _layer(x_sample, mod[bp:], state_conv[0], (cache_k[0], cache_v[0], cache_idx_k[0]), w,
                                 tm=ts, tq=ts, tk=512, peer_nbt=bs, peer_tt=ts, peer_ic=8)
    return (yp, ys, kp[None], vp[None], kip[None], cp[None], ks[None], vs[None], kis[None], cs[None])
```
